```python
import math
import jax, jax.numpy as jnp
from jax import lax
import numpy as np

D_MODEL = 2048
BATCH = 4
SEQ = 2048
DEPTH = 1
DEC_BATCH = 8
DEC_SEQ = 1
PAST_LEN = 16384
PAGE_SIZE = 128

ATT_HD = 64
ATT_HEADS_PER_GROUP = 8
DIL_PAIRS = ((128, 1), (512, 4), (2048, 16))
N_DIL = len(DIL_PAIRS)
ATT_WIDTH = N_DIL * ATT_HEADS_PER_GROUP * ATT_HD
QBLK = 128
CHUNK = 128
GM_HEADS = 4
GM_HD = 128
GM_WIDTH = GM_HEADS * GM_HD
MIX_WIDTH = ATT_WIDTH + GM_WIDTH
IN_WIDTH = 3 * ATT_WIDTH + 2 * GM_WIDTH
N_KEYS = 128
N_EXPERTS = N_KEYS * N_KEYS
PEER_HEADS = 8
PEER_DKEY = 256
PEER_TOPK = 16
PEER_TOK_BLOCK = 128
EPS = 1e-6

kernel_name = "hybrid_dilattn_gmlp_peer_decoder_step"


def rms_norm(x, g):
    xf = x.astype(jnp.float32)
    y = xf * lax.rsqrt(jnp.mean(xf * xf, axis=-1, keepdims=True) + EPS)
    return (y * g.astype(jnp.float32)).astype(x.dtype)


def layer_norm(x, g, b):
    xf = x.astype(jnp.float32)
    mu = jnp.mean(xf, axis=-1, keepdims=True)
    var = jnp.mean(jnp.square(xf - mu), axis=-1, keepdims=True)
    return (xf - mu) * lax.rsqrt(var + EPS) * g.astype(jnp.float32) + b.astype(jnp.float32)


def ada_mod(c, w_ada, b_ada):
    m = (jax.nn.silu(c) @ w_ada + b_ada)[:, None, :]
    return jnp.split(m, 6, axis=-1)


def dilated_attn_prompt(q, k, v, window, dil):
    B, S, H, hd = q.shape
    J = window // dil
    L = S // dil
    def fold(t):
        return t.reshape(B, L, dil, H, hd).transpose(0, 2, 1, 3, 4).reshape(B * dil, L, H, hd)
    nblk = -(-L // QBLK)
    Lp = nblk * QBLK
    pad = Lp - L
    qf = jnp.pad(fold(q), ((0, 0), (0, pad), (0, 0), (0, 0))).astype(jnp.float32)
    kf = jnp.pad(fold(k), ((0, 0), (QBLK, pad), (0, 0), (0, 0))).astype(jnp.float32)
    vf = jnp.pad(fold(v), ((0, 0), (QBLK, pad), (0, 0), (0, 0))).astype(jnp.float32)
    Z = B * dil
    qb = qf.reshape(Z, nblk, QBLK, H, hd)
    def band(t):
        prev = t[:, :Lp].reshape(Z, nblk, QBLK, H, hd)
        cur = t[:, QBLK:].reshape(Z, nblk, QBLK, H, hd)
        return jnp.concatenate([prev, cur], axis=2)
    kb, vb = band(kf), band(vf)
    s = jnp.einsum('znqhd,znkhd->znhqk', qb, kb) * (hd ** -0.5)
    qi = jnp.arange(QBLK)[:, None]
    kk = jnp.arange(2 * QBLK)[None, :]
    dist = qi + QBLK - kk
    kpos = jnp.arange(nblk)[:, None, None] * QBLK - QBLK + kk[None]
    mask = (dist >= 0)[None] & (dist <= J)[None] & (kpos >= 0)
    s = jnp.where(mask[None, :, None], s, -jnp.inf)
    m = jnp.max(s, axis=-1, keepdims=True)
    p = jnp.exp(s - m)
    den = jnp.sum(p, axis=-1, keepdims=True)
    o = jnp.einsum('znhqk,znkhd->znqhd', p / den, vb)
    lse = (m + jnp.log(den))[..., 0].transpose(0, 1, 3, 2)
    o = o.reshape(Z, Lp, H, hd)[:, :L].reshape(B, dil, L, H, hd).transpose(0, 2, 1, 3, 4).reshape(B, S, H, hd)
    lse = lse.reshape(Z, Lp, H)[:, :L].reshape(B, dil, L, H).transpose(0, 2, 1, 3).reshape(B, S, H)
    return o, lse


def dilated_attn_sample(q, k, v, buf, window, dil):
    T = q.shape[1]
    hd = q.shape[-1]
    nbuf = buf.shape[1]
    J = window // dil
    kall = jnp.concatenate([buf[:, :, 0].astype(jnp.float32), k.astype(jnp.float32)], axis=1)
    vall = jnp.concatenate([buf[:, :, 1].astype(jnp.float32), v.astype(jnp.float32)], axis=1)
    idx = nbuf + jnp.arange(T)[:, None] - dil * jnp.arange(J + 1)[None, :]
    valid = idx >= 0
    idxc = jnp.maximum(idx, 0)
    kg = kall[:, idxc]
    vg = vall[:, idxc]
    s = jnp.einsum('bthd,btjhd->bthj', q.astype(jnp.float32), kg) * (hd ** -0.5)
    s = jnp.where(valid[None, :, None, :], s, -jnp.inf)
    m = jnp.max(s, axis=-1, keepdims=True)
    p = jnp.exp(s - m)
    den = jnp.sum(p, axis=-1, keepdims=True)
    o = jnp.einsum('bthj,btjhd->bthd', p / den, vg)
    lse = (m + jnp.log(den))[..., 0]
    return o, lse


def gmlp_spatial(v, w_s, b_s, prompt):
    tril = jnp.tril(jnp.ones((CHUNK, CHUNK), dtype=jnp.float32))
    w = w_s.astype(jnp.float32) * tril
    B, S, Hg, C = v.shape
    if prompt:
        vc = v.reshape(B, S // CHUNK, CHUNK, Hg, C)
        z = jnp.einsum('hqp,bnphc->bnqhc', w, vc) + b_s.astype(jnp.float32).T[:, :, None]
        return z.reshape(B, S, Hg, C)
    z = jnp.einsum('hqp,bphc->bqhc', w[:, :S, :S], v) + b_s.astype(jnp.float32)[:, :S].T[:, :, None]
    return z


def token_mixers(h, w_in, gm_ln_g, gm_ln_b, w_s, b_s, bufs):
    B, S, _ = h.shape
    proj = h @ w_in
    q, k, v, gu, gv = jnp.split(proj, [ATT_WIDTH, 2 * ATT_WIDTH, 3 * ATT_WIDTH, 3 * ATT_WIDTH + GM_WIDTH], axis=-1)
    shp = (B, S, N_DIL, ATT_HEADS_PER_GROUP, ATT_HD)
    q, k, v = q.reshape(shp), k.reshape(shp), v.reshape(shp)
    outs, lses, kv_rows = [], [], []
    for g, (win, dil) in enumerate(DIL_PAIRS):
        qg, kg, vg = q[:, :, g], k[:, :, g], v[:, :, g]
        if bufs is None:
            o, lse = dilated_attn_prompt(qg, kg, vg, win, dil)
            keep = min(win, S)
            kv_rows.append(jnp.stack([kg[:, S - keep:], vg[:, S - keep:]], axis=2))
        else:
            o, lse = dilated_attn_sample(qg, kg, vg, bufs[g], win, dil)
            kv_rows.append(jnp.stack([kg, vg], axis=2))
        outs.append(o)
        lses.append(lse)
    alpha = jax.nn.softmax(jnp.stack(lses, axis=0), axis=0)
    att = jnp.stack([alpha[g][..., None] * outs[g] for g in range(N_DIL)], axis=2)
    att = att.reshape(B, S, ATT_WIDTH).astype(h.dtype)
    gu = jax.nn.gelu(gu).reshape(B, S, GM_HEADS, GM_HD).astype(jnp.float32)
    gvn = layer_norm(jax.nn.gelu(gv).reshape(B, S, GM_HEADS, GM_HD), gm_ln_g, gm_ln_b)
    z = gmlp_spatial(gvn, w_s, b_s, bufs is None)
    gm = (gu * z).reshape(B, S, GM_WIDTH).astype(h.dtype)
    return jnp.concatenate([att, gm], axis=-1), kv_rows, gvn.astype(h.dtype)


def peer(h, w_q, sub_keys, expert_u, expert_v):
    B, S, D = h.shape
    n = B * S
    xt = h.reshape(n, D)
    qr = (xt @ w_q).reshape(n, PEER_HEADS, 2, PEER_DKEY // 2).astype(jnp.float32)
    sc = jnp.einsum('nhpc,pkc->nhpk', qr, sub_keys.astype(jnp.float32))
    s1, i1 = lax.top_k(sc[:, :, 0], PEER_TOPK)
    s2, i2 = lax.top_k(sc[:, :, 1], PEER_TOPK)
    cand = (s1[..., :, None] + s2[..., None, :]).reshape(n, PEER_HEADS, PEER_TOPK * PEER_TOPK)
    cidx = (i1[..., :, None] * N_KEYS + i2[..., None, :]).reshape(n, PEER_HEADS, PEER_TOPK * PEER_TOPK)
    top_s, pos = lax.top_k(cand, PEER_TOPK)
    eidx = jnp.take_along_axis(cidx, pos, axis=-1)
    gate = jax.nn.softmax(top_s, axis=-1)
    blk = min(PEER_TOK_BLOCK, n)
    nb = -(-n // blk)
    pad = nb * blk - n
    xp = jnp.pad(xt, ((0, pad), (0, 0))).reshape(nb, blk, D)
    ep = jnp.pad(eidx, ((0, pad), (0, 0), (0, 0))).reshape(nb, blk, PEER_HEADS, PEER_TOPK)
    gp = jnp.pad(gate, ((0, pad), (0, 0), (0, 0))).reshape(nb, blk, PEER_HEADS, PEER_TOPK)
    def one_block(args):
        xb, eb, gb = args
        ub = expert_u[eb].astype(jnp.float32)
        a = jax.nn.gelu(jnp.einsum('td,thkd->thk', xb.astype(jnp.float32), ub))
        vb = expert_v[eb].astype(jnp.float32)
        return jnp.einsum('thk,thkd->td', gb * a, vb)
    out = lax.map(one_block, (xp, ep, gp))
    return out.reshape(nb * blk, D)[:n].reshape(B, S, D).astype(h.dtype)


def decoder_layer(x, c, wl, bufs):
    (w_ada, b_ada, norm_g, w_in, gm_ln_g, gm_ln_b, w_s, b_s, w_out, w_q, sub_keys, expert_u, expert_v) = wl
    sh1, sc1, gt1, sh2, sc2, gt2 = ada_mod(c, w_ada, b_ada)
    h = rms_norm(x, norm_g[0]) * (1 + sc1) + sh1
    mix, kv_rows, gv = token_mixers(h, w_in, gm_ln_g, gm_ln_b, w_s, b_s, bufs)
    x = x + gt1 * rms_norm(mix @ w_out, norm_g[1])
    h = rms_norm(x, norm_g[2]) * (1 + sc2) + sh2
    x = x + gt2 * rms_norm(peer(h, w_q, sub_keys, expert_u, expert_v), norm_g[3])
    return x, kv_rows, gv


def setup_inputs(seed: int = 0) -> dict:
    key = jax.random.key(seed)
    ks = jax.random.split(key, 22)
    D = D_MODEL
    def nrm(k, shape, s):
        return jax.random.normal(k, shape, jnp.float32) * s
    nbuf = [min(w, PAST_LEN) for w, _ in DIL_PAIRS]
    return {
        "x_prompt": nrm(ks[0], (BATCH, SEQ, D), 1.0),
        "x_sample": nrm(ks[1], (DEC_BATCH, DEC_SEQ, D), 1.0),
        "c_prompt": nrm(ks[2], (BATCH, D), 1.0),
        "c_sample": nrm(ks[3], (DEC_BATCH, D), 1.0),
        "cache_kv_dil1": nrm(ks[4], (DEPTH, DEC_BATCH, nbuf[0], 2, ATT_HEADS_PER_GROUP, ATT_HD), 1.0),
        "cache_kv_dil4": nrm(ks[5], (DEPTH, DEC_BATCH, nbuf[1], 2, ATT_HEADS_PER_GROUP, ATT_HD), 1.0),
        "cache_kv_dil16": nrm(ks[6], (DEPTH, DEC_BATCH, nbuf[2], 2, ATT_HEADS_PER_GROUP, ATT_HD), 1.0),
        "w_ada": nrm(ks[7], (DEPTH, D, 6 * D), 0.5 * D ** -0.5),
        "b_ada": nrm(ks[8], (DEPTH, 6 * D), 0.02),
        "norm_g": 1.0 + nrm(ks[9], (DEPTH, 4, D), 0.05),
        "w_in": nrm(ks[10], (DEPTH, D, IN_WIDTH), D ** -0.5),
        "gm_ln_g": 1.0 + nrm(ks[11], (DEPTH, GM_HEADS, GM_HD), 0.05),
        "gm_ln_b": nrm(ks[12], (DEPTH, GM_HEADS, GM_HD), 0.02),
        "w_s": nrm(ks[13], (DEPTH, GM_HEADS, CHUNK, CHUNK), CHUNK ** -0.5),
        "b_s": 1.0 + nrm(ks[14], (DEPTH, GM_HEADS, CHUNK), 0.1),
        "w_out": nrm(ks[15], (DEPTH, MIX_WIDTH, D), MIX_WIDTH ** -0.5),
        "w_q": nrm(ks[16], (DEPTH, D, PEER_HEADS * PEER_DKEY), D ** -0.5),
        "sub_keys": nrm(ks[17], (DEPTH, 2, N_KEYS, PEER_DKEY // 2), (PEER_DKEY // 2) ** -0.5),
        "expert_u": nrm(ks[18], (DEPTH, N_EXPERTS, D), D ** -0.5),
        "expert_v": nrm(ks[19], (DEPTH, N_EXPERTS, D), 1.0),
    }


def reference(x_prompt, x_sample, c_prompt, c_sample, cache_kv_dil1, cache_kv_dil4, cache_kv_dil16,
              w_ada, b_ada, norm_g, w_in, gm_ln_g, gm_ln_b, w_s, b_s, w_out, w_q, sub_keys,
              expert_u, expert_v):
    yp, ys = x_prompt, x_sample
    kvp = [[] for _ in DIL_PAIRS]
    kvs = [[] for _ in DIL_PAIRS]
    gvs = []
    for l in range(DEPTH):
        wl = (w_ada[l], b_ada[l], norm_g[l], w_in[l], gm_ln_g[l], gm_ln_b[l], w_s[l], b_s[l],
              w_out[l], w_q[l], sub_keys[l], expert_u[l], expert_v[l])
        yp, rows_p, _ = decoder_layer(yp, c_prompt, wl, None)
        ys, rows_s, gv = decoder_layer(ys, c_sample, wl, (cache_kv_dil1[l], cache_kv_dil4[l], cache_kv_dil16[l]))
        for g in range(N_DIL):
            kvp[g].append(rows_p[g])
            kvs[g].append(rows_s[g])
        gvs.append(gv)
    new_kv_dil1_prompt = jnp.stack(kvp[0], axis=0)
    new_kv_dil4_prompt = jnp.stack(kvp[1], axis=0)
    new_kv_dil16_prompt = jnp.stack(kvp[2], axis=0)
    new_kv_dil1_sample = jnp.stack(kvs[0], axis=0)
    new_kv_dil4_sample = jnp.stack(kvs[1], axis=0)
    new_kv_dil16_sample = jnp.stack(kvs[2], axis=0)
    state_gmlp_v_sample = jnp.stack(gvs, axis=0)
    return (yp, ys, new_kv_dil1_prompt, new_kv_dil4_prompt, new_kv_dil16_prompt,
            new_kv_dil1_sample, new_kv_dil4_sample, new_kv_dil16_sample, state_gmlp_v_sample)
```

```python
import functools
import math

import jax
import jax.numpy as jnp
from jax import lax
from jax.experimental import pallas as pl
from jax.experimental.pallas import tpu as pltpu

F32 = jnp.float32
BF16 = jnp.bfloat16

D_MODEL = 2048
ATT_HD = 64
ATT_HEADS = 8
GROUP_W = ATT_HEADS * ATT_HD
DIL_PAIRS = ((128, 1), (512, 4), (2048, 16))
N_DIL = 3
ATT_WIDTH = N_DIL * GROUP_W
QBLK = 128
BAND = 128
CHUNK = 128
GM_HEADS = 4
GM_HD = 128
GM_WIDTH = GM_HEADS * GM_HD
IN_WIDTH = 3 * ATT_WIDTH + 2 * GM_WIDTH
N_KEYS = 128
N_EXPERTS = N_KEYS * N_KEYS
PEER_HEADS = 8
PEER_HALF = 128
PEER_TOPK = 16
EPS = 1e-6
NEG_INF = float("-inf")

LANES = 128
VMEM_LIMIT = 56 * 1024 * 1024

Q_OFF = 0
KV_OFF = ATT_WIDTH
GU_OFF = ATT_WIDTH + N_DIL * 2 * GROUP_W
GV_OFF = GU_OFF + GM_WIDTH


def _cparams(sem):
    return pltpu.CompilerParams(dimension_semantics=sem, vmem_limit_bytes=VMEM_LIMIT)


def _gelu(x):
    c = math.sqrt(2.0 / math.pi)
    return 0.5 * x * (1.0 + jnp.tanh(c * (x + 0.044715 * (x * x * x))))


def _rms(x, g):
    return x * lax.rsqrt(jnp.mean(x * x, axis=-1, keepdims=True) + EPS) * g


def _dot(a, b):
    return jnp.dot(a, b, preferred_element_type=F32)


def _dot_nt(a, b):
    return lax.dot_general(a, b, (((1,), (1,)), ((), ())), preferred_element_type=F32)


def _ada_kernel(c_ref, w_ref, b_ref, o_ref):
    c = c_ref[...]
    act = (c / (1.0 + jnp.exp(-c))).astype(BF16)
    o_ref[...] = _dot(act, w_ref[...].astype(BF16)) + b_ref[...]


def _ada(c_all, w_ada, b_ada):
    rows = c_all.shape[0]
    n_out = w_ada.shape[1]
    tn = 1024
    return pl.pallas_call(
        _ada_kernel,
        grid=(n_out // tn,),
        in_specs=[
            pl.BlockSpec((rows, D_MODEL), lambda j: (0, 0)),
            pl.BlockSpec((D_MODEL, tn), lambda j: (0, j)),
            pl.BlockSpec((1, tn), lambda j: (0, j)),
        ],
        out_specs=pl.BlockSpec((rows, tn), lambda j: (0, j)),
        out_shape=jax.ShapeDtypeStruct((rows, n_out), F32),
        compiler_params=_cparams(("parallel",)),
        name="ada_mod",
    )(c_all, w_ada, b_ada)


def _modulated_norm(x_ref, sh_ref, sc_ref, g_ref):
    x = x_ref[...]
    return _rms(x, g_ref[...]) * (1.0 + sc_ref[...]) + sh_ref[...]


def _gm_layernorm(gv, lng_ref, lnb_ref, hh):
    v = gv[:, hh * GM_HD:(hh + 1) * GM_HD]
    mu = jnp.mean(v, axis=-1, keepdims=True)
    var = jnp.mean(jnp.square(v - mu), axis=-1, keepdims=True)
    return (v - mu) * lax.rsqrt(var + EPS) * lng_ref[pl.ds(hh, 1), :] + lnb_ref[pl.ds(hh, 1), :]


def _inproj_prompt_kernel(x_ref, sh_ref, sc_ref, g_ref, w_ref, ws_ref, bsb_ref, lng_ref, lnb_ref,
                          q0_ref, k0_ref, v0_ref, q1_ref, k1_ref, v1_ref, q2_ref, k2_ref, v2_ref,
                          kv0_ref, kv1_ref, kv2_ref, gm_ref, h_scr, y_scr):
    tm = x_ref.shape[0]
    h_scr[...] = _modulated_norm(x_ref, sh_ref, sc_ref, g_ref).astype(BF16)

    def proj(off):
        return _dot(h_scr[...], w_ref[:, off:off + GROUP_W])

    folded = ((q0_ref, k0_ref, v0_ref), (q1_ref, k1_ref, v1_ref), (q2_ref, k2_ref, v2_ref))
    kv_refs = (kv0_ref, kv1_ref, kv2_ref)
    for g, (_, dil) in enumerate(DIL_PAIRS):
        offs = (Q_OFF + g * GROUP_W, KV_OFF + g * 2 * GROUP_W, KV_OFF + g * 2 * GROUP_W + GROUP_W)
        for which, off in enumerate(offs):
            y = proj(off)
            if which > 0:
                kv_refs[g][:, (which - 1) * GROUP_W:which * GROUP_W] = y
            else:
                y = y * (ATT_HD ** -0.5)
            dst = folded[g][which]
            if dil == 1:
                dst[...] = y.astype(BF16)
            else:
                for cb in range(GROUP_W // LANES):
                    y_scr[cb] = y[:, cb * LANES:(cb + 1) * LANES]
                for r in range(dil):
                    for cb in range(GROUP_W // LANES):
                        dst[0, r, :, cb * LANES:(cb + 1) * LANES] = (
                            y_scr[cb, pl.ds(r, tm // dil, stride=dil), :].astype(BF16))

    gu = _gelu(proj(GU_OFF))
    gv = _gelu(proj(GV_OFF))
    row = lax.broadcasted_iota(jnp.int32, (CHUNK, CHUNK), 0)
    col = lax.broadcasted_iota(jnp.int32, (CHUNK, CHUNK), 1)
    for hh in range(GM_HEADS):
        gvn = _gm_layernorm(gv, lng_ref, lnb_ref, hh).astype(BF16)
        w_tril = jnp.where(row >= col, ws_ref[hh], 0.0).astype(BF16)
        for ci in range(tm // CHUNK):
            z = _dot(w_tril, gvn[ci * CHUNK:(ci + 1) * CHUNK]) + bsb_ref[hh]
            gu_blk = gu[ci * CHUNK:(ci + 1) * CHUNK, hh * GM_HD:(hh + 1) * GM_HD]
            gm_ref[ci * CHUNK:(ci + 1) * CHUNK, hh * GM_HD:(hh + 1) * GM_HD] = (gu_blk * z).astype(BF16)


def _inproj_prompt(x, mod, norm_g, w_in_p, w_s, bsb, ln_g, ln_b, batch, seq):
    n = x.shape[0]
    tm = 256
    tiles_per_seq = seq // tm

    def mod_spec(k):
        return pl.BlockSpec((None, 1, D_MODEL), lambda i: ((i // tiles_per_seq) * 6 + k, 0, 0))

    def folded_spec(dil):
        if dil == 1:
            return pl.BlockSpec((tm, GROUP_W), lambda i: (i, 0))
        return pl.BlockSpec((1, dil, tm // dil, GROUP_W),
                            lambda i: (i // tiles_per_seq, 0, i % tiles_per_seq, 0))

    def folded_shape(dil):
        if dil == 1:
            return jax.ShapeDtypeStruct((n, GROUP_W), BF16)
        return jax.ShapeDtypeStruct((batch, dil, seq // dil, GROUP_W), BF16)

    out_specs, out_shape = [], []
    for _, dil in DIL_PAIRS:
        for _ in range(3):
            out_specs.append(folded_spec(dil))
            out_shape.append(folded_shape(dil))
    for _ in range(N_DIL):
        out_specs.append(pl.BlockSpec((tm, 2 * GROUP_W), lambda i: (i, 0)))
        out_shape.append(jax.ShapeDtypeStruct((n, 2 * GROUP_W), F32))
    out_specs.append(pl.BlockSpec((tm, GM_WIDTH), lambda i: (i, 0)))
    out_shape.append(jax.ShapeDtypeStruct((n, GM_WIDTH), BF16))

    return pl.pallas_call(
        _inproj_prompt_kernel,
        grid=(n // tm,),
        in_specs=[
            pl.BlockSpec((tm, D_MODEL), lambda i: (i, 0)),
            mod_spec(0), mod_spec(1),
            pl.BlockSpec((None, 1, D_MODEL), lambda i: (0, 0, 0)),
            pl.BlockSpec((D_MODEL, IN_WIDTH), lambda i: (0, 0)),
            pl.BlockSpec((GM_HEADS, CHUNK, CHUNK), lambda i: (0, 0, 0)),
            pl.BlockSpec((GM_HEADS, CHUNK, GM_HD), lambda i: (0, 0, 0)),
            pl.BlockSpec((GM_HEADS, GM_HD), lambda i: (0, 0)),
            pl.BlockSpec((GM_HEADS, GM_HD), lambda i: (0, 0)),
        ],
        out_specs=out_specs,
        out_shape=out_shape,
        scratch_shapes=[pltpu.VMEM((tm, D_MODEL), BF16), pltpu.VMEM((GROUP_W // LANES, tm, LANES), F32)],
        compiler_params=_cparams(("parallel",)),
        name="inproj_prompt",
    )(x, mod, mod, norm_g, w_in_p, w_s, bsb, ln_g, ln_b)


def _inproj_sample_kernel(x_ref, sh_ref, sc_ref, g_ref, w_ref, wd_ref, b0_ref, lng_ref, lnb_ref,
                          q_ref, kv0_ref, kv1_ref, kv2_ref, gm_ref, gvn_ref):
    h = _modulated_norm(x_ref, sh_ref, sc_ref, g_ref).astype(BF16)
    q_ref[...] = _dot(h, w_ref[:, Q_OFF:Q_OFF + ATT_WIDTH])
    for g, kv_ref in enumerate((kv0_ref, kv1_ref, kv2_ref)):
        off = KV_OFF + g * 2 * GROUP_W
        kv_ref[...] = _dot(h, w_ref[:, off:off + 2 * GROUP_W])
    gu = _gelu(_dot(h, w_ref[:, GU_OFF:GU_OFF + GM_WIDTH]))
    gv = _gelu(_dot(h, w_ref[:, GV_OFF:GV_OFF + GM_WIDTH]))
    for hh in range(GM_HEADS):
        sl = slice(hh * GM_HD, (hh + 1) * GM_HD)
        gvn = _gm_layernorm(gv, lng_ref, lnb_ref, hh)
        gvn_ref[:, sl] = gvn
        z = wd_ref[:, sl] * gvn + b0_ref[:, sl]
        gm_ref[:, sl] = (gu[:, sl] * z).astype(BF16)


def _inproj_sample(x, sh, sc, norm_g, w_in_p, wd, b0, ln_g, ln_b):
    n = x.shape[0]
    full = lambda shape: pl.BlockSpec(shape, lambda i: tuple(0 for _ in shape))
    return pl.pallas_call(
        _inproj_sample_kernel,
        grid=(1,),
        in_specs=[
            full((n, D_MODEL)), full((n, D_MODEL)), full((n, D_MODEL)),
            pl.BlockSpec((None, 1, D_MODEL), lambda i: (0, 0, 0)),
            full((D_MODEL, IN_WIDTH)),
            full((1, GM_WIDTH)), full((1, GM_WIDTH)),
            full((GM_HEADS, GM_HD)), full((GM_HEADS, GM_HD)),
        ],
        out_specs=[full((n, ATT_WIDTH)), full((n, 2 * GROUP_W)), full((n, 2 * GROUP_W)),
                   full((n, 2 * GROUP_W)), full((n, GM_WIDTH)), full((n, GM_WIDTH))],
        out_shape=[jax.ShapeDtypeStruct((n, ATT_WIDTH), F32)]
        + [jax.ShapeDtypeStruct((n, 2 * GROUP_W), F32)] * 3
        + [jax.ShapeDtypeStruct((n, GM_WIDTH), BF16), jax.ShapeDtypeStruct((n, GM_WIDTH), F32)],
        compiler_params=_cparams(("arbitrary",)),
        name="inproj_sample",
    )(x, sh, sc, norm_g, w_in_p, wd, b0, ln_g, ln_b)


def _attn_block(qt, kt, vt, off, nkeys):
    lane = lax.broadcasted_iota(jnp.int32, (1, LANES), 1)
    lo = lane < ATT_HD
    qi = lax.broadcasted_iota(jnp.int32, (QBLK, nkeys), 0)
    kk = lax.broadcasted_iota(jnp.int32, (QBLK, nkeys), 1)
    dist = qi - kk + off
    valid = lax.bitcast_convert_type(dist, jnp.uint32) <= jnp.uint32(BAND)
    res = []
    for sel in (lo, jnp.logical_not(lo)):
        qm = jnp.where(sel, qt, jnp.zeros_like(qt))
        s = jnp.where(valid, _dot_nt(qm, kt), NEG_INF)
        m = jnp.max(s, axis=-1, keepdims=True)
        p = jnp.exp(s - m)
        den = jnp.sum(p, axis=-1, keepdims=True)
        o = _dot(p.astype(BF16), vt) * (1.0 / den)
        res.append((o, m + jnp.log(den)))
    o = jnp.where(lo, res[0][0], res[1][0])
    lse = jnp.where(lo, res[0][1], res[1][1])
    return o, lse


def _attn_prompt_kernel(q0_ref, k0_ref, v0_ref, q1_ref, k1_ref, v1_ref, q2_ref, k2_ref, v2_ref,
                        a0_ref, a1_ref, a2_ref, of_scr, lf_scr, to_scr, tl_scr):
    seq = q0_ref.shape[0]

    def run_block(g, qt, kt, vt, off, nkeys, row0):
        o, lse = _attn_block(qt, kt, vt, off, nkeys)
        of_scr[g, pl.ds(row0, QBLK), :] = o
        lf_scr[g, pl.ds(row0, QBLK), :] = lse

    def g0_body(qb, carry):
        qs = pl.multiple_of(qb * QBLK, QBLK)
        ks = pl.multiple_of(jnp.maximum(qb - 1, 0) * QBLK, QBLK)
        run_block(0, q0_ref[pl.ds(qs, QBLK), :], k0_ref[pl.ds(ks, 2 * QBLK), :],
                  v0_ref[pl.ds(ks, 2 * QBLK), :], qs - ks, 2 * QBLK, qs)
        return carry

    lax.fori_loop(0, seq // QBLK, g0_body, 0)

    dil1 = DIL_PAIRS[1][1]
    len1 = seq // dil1

    def g1_body(it, carry):
        r = it // (len1 // QBLK)
        qb = it % (len1 // QBLK)
        qs = pl.multiple_of(qb * QBLK, QBLK)
        ks = pl.multiple_of(jnp.maximum(qb - 1, 0) * QBLK, QBLK)
        run_block(1, q1_ref[0, r, pl.ds(qs, QBLK), :], k1_ref[0, r, pl.ds(ks, 2 * QBLK), :],
                  v1_ref[0, r, pl.ds(ks, 2 * QBLK), :], qs - ks, 2 * QBLK,
                  pl.multiple_of(r * len1 + qs, QBLK))
        return carry

    lax.fori_loop(0, dil1 * (len1 // QBLK), g1_body, 0)

    dil2 = DIL_PAIRS[2][1]
    len2 = seq // dil2

    def g2_body(r, carry):
        run_block(2, q2_ref[0, r], k2_ref[0, r], v2_ref[0, r], 0, QBLK,
                  pl.multiple_of(r * len2, QBLK))
        return carry

    lax.fori_loop(0, dil2, g2_body, 0)

    def mix_body(nt, carry):
        t0 = pl.multiple_of(nt * QBLK, QBLK)
        o_nat = [of_scr[0, pl.ds(t0, QBLK), :]]
        l_nat = [lf_scr[0, pl.ds(t0, QBLK), :]]
        for g, dil, length in ((1, dil1, len1), (2, dil2, len2)):
            per = QBLK // dil
            for r in range(dil):
                src = pl.multiple_of(r * length + nt * per, per)
                to_scr[g - 1, pl.ds(r, per, stride=dil), :] = of_scr[g, pl.ds(src, per), :]
                tl_scr[g - 1, pl.ds(r, per, stride=dil), :] = lf_scr[g, pl.ds(src, per), :]
            o_nat.append(to_scr[g - 1])
            l_nat.append(tl_scr[g - 1])
        mx = jnp.maximum(jnp.maximum(l_nat[0], l_nat[1]), l_nat[2])
        e = [jnp.exp(l - mx) for l in l_nat]
        inv = 1.0 / (e[0] + e[1] + e[2])
        for g, a_ref in enumerate((a0_ref, a1_ref, a2_ref)):
            a_ref[pl.ds(t0, QBLK), :] = (o_nat[g] * (e[g] * inv)).astype(BF16)
        return carry

    lax.fori_loop(0, seq // QBLK, mix_body, 0)


def _attn_prompt(folded, batch, seq):
    n = batch * seq
    pairs = GROUP_W // LANES
    in_specs = []
    for _, dil in DIL_PAIRS:
        for _ in range(3):
            if dil == 1:
                in_specs.append(pl.BlockSpec((seq, LANES), lambda b, hp: (b, hp)))
            else:
                in_specs.append(pl.BlockSpec((1, dil, seq // dil, LANES), lambda b, hp: (b, 0, 0, hp)))
    return pl.pallas_call(
        _attn_prompt_kernel,
        grid=(batch, pairs),
        in_specs=in_specs,
        out_specs=[pl.BlockSpec((seq, LANES), lambda b, hp: (b, hp))] * N_DIL,
        out_shape=[jax.ShapeDtypeStruct((n, GROUP_W), BF16)] * N_DIL,
        scratch_shapes=[pltpu.VMEM((N_DIL, seq, LANES), F32), pltpu.VMEM((N_DIL, seq, LANES), F32),
                        pltpu.VMEM((N_DIL - 1, QBLK, LANES), F32), pltpu.VMEM((N_DIL - 1, QBLK, LANES), F32)],
        compiler_params=_cparams(("parallel", "parallel")),
        name="attn_prompt",
    )(*folded)


def _attn_sample_kernel(q_ref, kv0_ref, kv1_ref, kv2_ref, c0_ref, c1_ref, c2_ref, ind_ref, indt_ref,
                        a0_ref, a1_ref, a2_ref):
    b = pl.program_id(0)
    hi = lax.Precision.HIGHEST
    ind = ind_ref[...]
    indt = indt_ref[...]

    def expand(v):
        v8 = jnp.broadcast_to(v, (8, LANES))
        return jnp.dot(v8, indt, precision=hi, preferred_element_type=F32)[0:1]

    outs, lses = [], []
    for g, (kv_ref, c_ref) in enumerate(((kv0_ref, c0_ref), (kv1_ref, c1_ref), (kv2_ref, c2_ref))):
        q = q_ref[pl.ds(b, 1), g * GROUP_W:(g + 1) * GROUP_W]
        k_new = kv_ref[pl.ds(b, 1), 0:GROUP_W]
        v_new = kv_ref[pl.ds(b, 1), GROUP_W:2 * GROUP_W]
        k_buf = c_ref[0, :, 0:GROUP_W]
        v_buf = c_ref[0, :, GROUP_W:2 * GROUP_W]
        scale = ATT_HD ** -0.5
        s_buf = jnp.dot(k_buf * q, ind, precision=hi, preferred_element_type=F32) * scale
        s_new = jnp.dot(jnp.broadcast_to(k_new * q, (8, GROUP_W)), ind, precision=hi,
                        preferred_element_type=F32)[0:1] * scale
        m = jnp.maximum(jnp.max(s_buf, axis=0, keepdims=True), s_new)
        p_buf = jnp.exp(s_buf - m)
        p_new = jnp.exp(s_new - m)
        den = jnp.sum(p_buf, axis=0, keepdims=True) + p_new
        p_exp = jnp.dot(p_buf, indt, precision=hi, preferred_element_type=F32)
        o = (jnp.sum(p_exp * v_buf, axis=0, keepdims=True) + expand(p_new) * v_new) / expand(den)
        outs.append(o)
        lses.append(expand(m + jnp.log(den)))
    mx = jnp.maximum(jnp.maximum(lses[0], lses[1]), lses[2])
    e = [jnp.exp(l - mx) for l in lses]
    inv = 1.0 / (e[0] + e[1] + e[2])
    for g, a_ref in enumerate((a0_ref, a1_ref, a2_ref)):
        a_ref[0] = (outs[g] * (e[g] * inv)).astype(BF16)


def _attn_sample(q, kvs, caches, n_real):
    n = q.shape[0]
    lane_head = jnp.arange(GROUP_W, dtype=jnp.int32) // ATT_HD
    ind = (lane_head[:, None] == jnp.arange(LANES, dtype=jnp.int32)[None, :]).astype(F32)
    full = lambda shape: pl.BlockSpec(shape, lambda b: tuple(0 for _ in shape))
    row_w = 2 * GROUP_W
    cviews = [c.reshape(n_real, c.shape[1] // dil, dil * row_w) for c, (_, dil) in zip(caches, DIL_PAIRS)]
    return pl.pallas_call(
        _attn_sample_kernel,
        grid=(n_real,),
        in_specs=[full((n, ATT_WIDTH))] + [full((n, row_w))] * 3
        + [pl.BlockSpec((1, BAND, row_w), lambda b: (b, 0, 0))] * 3
        + [full((GROUP_W, LANES)), full((LANES, GROUP_W))],
        out_specs=[pl.BlockSpec((1, 1, GROUP_W), lambda b: (b, 0, 0))] * N_DIL,
        out_shape=[jax.ShapeDtypeStruct((n_real, 1, GROUP_W), BF16)] * N_DIL,
        compiler_params=_cparams(("arbitrary",)),
        name="attn_sample",
    )(q, *kvs, *cviews, ind, ind.T)


def _topk_desc(sc, k):
    vals = []
    cur = sc
    for i in range(k):
        m = jnp.max(cur, axis=0, keepdims=True)
        vals.append(m)
        if i + 1 < k:
            cur = jnp.where(cur == m, NEG_INF, cur)
    return vals


def _outproj_peer_kernel(x_ref, a0_ref, a1_ref, a2_ref, gm_ref, gt1_ref, sh2_ref, sc2_ref, g1_ref, g2_ref,
                         wo_ref, wq_ref, keys_ref,
                         x1_ref, h2t_ref, s1_ref, s2_ref, thr_ref, lse_ref):
    mo = _dot(a0_ref[...], wo_ref[0:GROUP_W, :])
    mo += _dot(a1_ref[...], wo_ref[GROUP_W:2 * GROUP_W, :])
    mo += _dot(a2_ref[...], wo_ref[2 * GROUP_W:3 * GROUP_W, :])
    mo += _dot(gm_ref[...], wo_ref[3 * GROUP_W:4 * GROUP_W, :])
    x1 = x_ref[...] + gt1_ref[...] * _rms(mo, g1_ref[...])
    x1_ref[...] = x1
    h2 = _rms(x1, g2_ref[...]) * (1.0 + sc2_ref[...]) + sh2_ref[...]
    h2t_ref[...] = jnp.transpose(h2).astype(BF16)
    qr = _dot(h2.astype(BF16), wq_ref[...])
    for h in range(PEER_HEADS):
        tops = []
        for p, s_ref in enumerate((s1_ref, s2_ref)):
            c0 = (h * 2 + p) * PEER_HALF
            sc = _dot_nt(keys_ref[p], qr[:, c0:c0 + PEER_HALF].astype(BF16))
            s_ref[h] = sc
            tops.append(_topk_desc(sc, PEER_TOPK))
        t1 = jnp.concatenate(tops[0], axis=0)
        cand = jnp.concatenate([t1 + tops[1][b] for b in range(PEER_TOPK)], axis=0)
        best = _topk_desc(cand, PEER_TOPK)
        z = jnp.ones_like(best[0])
        for v in best[1:]:
            z = z + jnp.exp(v - best[0])
        thr_ref[pl.ds(h, 1), :] = best[-1]
        lse_ref[pl.ds(h, 1), :] = best[0] + jnp.log(z)


def _outproj_peer(x, atts, gm, mods, norm_g, w_out, w_q, keys, tm, per_row_mod, tiles_per_seq):
    n = x.shape[0]
    if per_row_mod:
        mod_specs = [pl.BlockSpec((tm, D_MODEL), lambda i: (i, 0))] * 3
    else:
        mod_specs = [pl.BlockSpec((None, 1, D_MODEL), lambda i, k=k: ((i // tiles_per_seq) * 6 + k, 0, 0))
                     for k in (2, 3, 4)]
    row_spec = lambda w: pl.BlockSpec((tm, w), lambda i: (i, 0))
    const = lambda shape: pl.BlockSpec(shape, lambda i: tuple(0 for _ in shape))
    return pl.pallas_call(
        _outproj_peer_kernel,
        grid=(n // tm,),
        in_specs=[row_spec(D_MODEL)] + [row_spec(GROUP_W)] * 4 + mod_specs
        + [pl.BlockSpec((None, 1, D_MODEL), lambda i: (1, 0, 0)),
           pl.BlockSpec((None, 1, D_MODEL), lambda i: (2, 0, 0)),
           const((D_MODEL, D_MODEL)), const((D_MODEL, D_MODEL)), const((2, N_KEYS, PEER_HALF))],
        out_specs=[row_spec(D_MODEL),
                   pl.BlockSpec((D_MODEL, tm), lambda i: (0, i)),
                   pl.BlockSpec((PEER_HEADS, N_KEYS, tm), lambda i: (0, 0, i)),
                   pl.BlockSpec((PEER_HEADS, N_KEYS, tm), lambda i: (0, 0, i)),
                   pl.BlockSpec((PEER_HEADS, tm), lambda i: (0, i)),
                   pl.BlockSpec((PEER_HEADS, tm), lambda i: (0, i))],
        out_shape=[jax.ShapeDtypeStruct((n, D_MODEL), F32),
                   jax.ShapeDtypeStruct((D_MODEL, n), BF16),
                   jax.ShapeDtypeStruct((PEER_HEADS, N_KEYS, n), F32),
                   jax.ShapeDtypeStruct((PEER_HEADS, N_KEYS, n), F32),
                   jax.ShapeDtypeStruct((PEER_HEADS, n), F32),
                   jax.ShapeDtypeStruct((PEER_HEADS, n), F32)],
        compiler_params=_cparams(("parallel",)),
        name="outproj_peer",
    )(x, *atts, gm, *mods, norm_g, norm_g, w_out, w_q, keys)


def _expert_kernel(h2t_ref, u_ref, vt_ref, s1_ref, s2_ref, thr_ref, lse_ref, o_ref, acc_scr, act_scr, w_scr):
    j = pl.program_id(1)
    te = u_ref.shape[0]
    rows_per_tile = te // N_KEYS
    ch = 16

    @pl.when(j == 0)
    def _():
        acc_scr[...] = jnp.zeros_like(acc_scr)

    act_scr[...] = _gelu(_dot(u_ref[...], h2t_ref[...]))

    for rr in range(rows_per_tile):
        r = j * rows_per_tile + rr
        s1_rows = [s1_ref[h, pl.ds(r, 1), :] for h in range(PEER_HEADS)]
        shifted = [s1_rows[h] - lse_ref[pl.ds(h, 1), :] for h in range(PEER_HEADS)]
        thr_rows = [thr_ref[pl.ds(h, 1), :] for h in range(PEER_HEADS)]

        def chunk_body(cc, carry, rr=rr, s1_rows=s1_rows, shifted=shifted, thr_rows=thr_rows):
            c0 = pl.multiple_of(cc * ch, ch)
            gate = None
            for h in range(PEER_HEADS):
                s2 = s2_ref[h, pl.ds(c0, ch), :]
                term = jnp.where(s1_rows[h] + s2 >= thr_rows[h], jnp.exp(shifted[h] + s2), 0.0)
                gate = term if gate is None else gate + term
            e0 = pl.multiple_of(rr * N_KEYS + c0, ch)
            w_scr[pl.ds(e0, ch), :] = (gate * act_scr[pl.ds(e0, ch), :]).astype(BF16)
            return carry

        lax.fori_loop(0, N_KEYS // ch, chunk_body, 0)

    acc_scr[...] += _dot(vt_ref[...], w_scr[...])

    @pl.when(j == pl.num_programs(1) - 1)
    def _():
        o_ref[...] = acc_scr[...]


def _experts(h2t, u_bf, vt_bf, s1, s2, thr, lse, tm):
    n = h2t.shape[1]
    te = 512
    return pl.pallas_call(
        _expert_kernel,
        grid=(n // tm, N_EXPERTS // te),
        in_specs=[
            pl.BlockSpec((D_MODEL, tm), lambda i, j: (0, i)),
            pl.BlockSpec((te, D_MODEL), lambda i, j: (j, 0)),
            pl.BlockSpec((D_MODEL, te), lambda i, j: (0, j)),
            pl.BlockSpec((PEER_HEADS, N_KEYS, tm), lambda i, j: (0, 0, i)),
            pl.BlockSpec((PEER_HEADS, N_KEYS, tm), lambda i, j: (0, 0, i)),
            pl.BlockSpec((PEER_HEADS, tm), lambda i, j: (0, i)),
            pl.BlockSpec((PEER_HEADS, tm), lambda i, j: (0, i)),
        ],
        out_specs=pl.BlockSpec((D_MODEL, tm), lambda i, j: (0, i)),
        out_shape=jax.ShapeDtypeStruct((D_MODEL, n), F32),
        scratch_shapes=[pltpu.VMEM((D_MODEL, tm), F32), pltpu.VMEM((te, tm), F32), pltpu.VMEM((te, tm), BF16)],
        compiler_params=_cparams(("parallel", "arbitrary")),
        name="peer_experts",
    )(h2t, u_bf, vt_bf, s1, s2, thr, lse)


def _final_kernel(x1_ref, pt_ref, gt2_ref, g3_ref, y_ref):
    peer = jnp.transpose(pt_ref[...])
    y_ref[...] = x1_ref[...] + gt2_ref[...] * _rms(peer, g3_ref[...])


def _final(x1, peer_t, gt2, norm_g, tm, per_row_mod, tiles_per_seq):
    n = x1.shape[0]
    if per_row_mod:
        gt2_spec = pl.BlockSpec((tm, D_MODEL), lambda i: (i, 0))
    else:
        gt2_spec = pl.BlockSpec((None, 1, D_MODEL), lambda i: ((i // tiles_per_seq) * 6 + 5, 0, 0))
    return pl.pallas_call(
        _final_kernel,
        grid=(n // tm,),
        in_specs=[pl.BlockSpec((tm, D_MODEL), lambda i: (i, 0)),
                  pl.BlockSpec((D_MODEL, tm), lambda i: (0, i)),
                  gt2_spec,
                  pl.BlockSpec((None, 1, D_MODEL), lambda i: (3, 0, 0))],
        out_specs=pl.BlockSpec((tm, D_MODEL), lambda i: (i, 0)),
        out_shape=jax.ShapeDtypeStruct((n, D_MODEL), F32),
        compiler_params=_cparams(("parallel",)),
        name="final_residual",
    )(x1, peer_t, gt2, norm_g)


def _permute_w_in(w_in):
    q = w_in[:, :ATT_WIDTH]
    k = w_in[:, ATT_WIDTH:2 * ATT_WIDTH]
    v = w_in[:, 2 * ATT_WIDTH:3 * ATT_WIDTH]
    parts = [q]
    for g in range(N_DIL):
        parts += [k[:, g * GROUP_W:(g + 1) * GROUP_W], v[:, g * GROUP_W:(g + 1) * GROUP_W]]
    parts.append(w_in[:, 3 * ATT_WIDTH:])
    return jnp.concatenate(parts, axis=1).astype(BF16)


def kernel(x_prompt, x_sample, c_prompt, c_sample, cache_kv_dil1, cache_kv_dil4, cache_kv_dil16, w_ada, b_ada, norm_g, w_in, gm_ln_g, gm_ln_b, w_s, b_s, w_out, w_q, sub_keys, expert_u, expert_v):
    batch, seq, _ = x_prompt.shape
    n_s = x_sample.shape[0] * x_sample.shape[1]
    assert w_ada.shape[0] == 1 and x_sample.shape[1] == 1
    n_p = batch * seq
    pad_s = 128

    w_in_p = _permute_w_in(w_in[0])
    w_out_b = w_out[0].astype(BF16)
    w_q_b = w_q[0].astype(BF16)
    keys_b = sub_keys[0].astype(BF16)
    u_b = expert_u[0].astype(BF16)
    vt_b = jnp.transpose(expert_v[0]).astype(BF16)
    ng = norm_g[0].reshape(4, 1, D_MODEL)
    ln_g, ln_b = gm_ln_g[0], gm_ln_b[0]

    rows = batch + n_s
    c_all = jnp.zeros((16, D_MODEL), F32).at[:rows].set(jnp.concatenate([c_prompt, c_sample], axis=0))
    mod = _ada(c_all, w_ada[0], b_ada)
    mod_p = mod.reshape(16 * 6, 1, D_MODEL)
    mod_s = jnp.zeros((pad_s, 6, D_MODEL), F32).at[:n_s].set(mod[batch:rows].reshape(n_s, 6, D_MODEL))
    mods_s = [mod_s[:, k] for k in range(6)]

    xp = x_prompt.reshape(n_p, D_MODEL)
    bsb = jnp.broadcast_to(b_s[0][:, :, None], (GM_HEADS, CHUNK, GM_HD))
    outs = _inproj_prompt(xp, mod_p, ng, w_in_p, w_s[0], bsb, ln_g, ln_b, batch, seq)
    folded, kv_p, gm_p = outs[:9], outs[9:12], outs[12]
    att_p = _attn_prompt(folded, batch, seq)
    tm_p = 256
    x1_p, h2t_p, s1_p, s2_p, thr_p, lse_p = _outproj_peer(
        xp, att_p, gm_p, [mod_p] * 3, ng, w_out_b, w_q_b, keys_b, tm_p, False, seq // tm_p)
    peer_t_p = _experts(h2t_p, u_b, vt_b, s1_p, s2_p, thr_p, lse_p, 512)
    y_p = _final(x1_p, peer_t_p, mod_p, ng, tm_p, False, seq // tm_p)

    xs = jnp.zeros((pad_s, D_MODEL), F32).at[:n_s].set(x_sample.reshape(n_s, D_MODEL))
    wd = jnp.repeat(w_s[0][:, 0, 0], GM_HD)[None, :]
    b0 = jnp.repeat(b_s[0][:, 0], GM_HD)[None, :]
    q_s, kv0_s, kv1_s, kv2_s, gm_s, gvn_s = _inproj_sample(
        xs, mods_s[0], mods_s[1], ng, w_in_p, wd, b0, ln_g, ln_b)
    caches = (cache_kv_dil1[0].reshape(n_s, -1, 2 * GROUP_W), cache_kv_dil4[0].reshape(n_s, -1, 2 * GROUP_W),
              cache_kv_dil16[0].reshape(n_s, -1, 2 * GROUP_W))
    att_s = _attn_sample(q_s, (kv0_s, kv1_s, kv2_s), caches, n_s)
    att_s = [jnp.zeros((pad_s, GROUP_W), BF16).at[:n_s].set(a.reshape(n_s, GROUP_W)) for a in att_s]
    x1_s, h2t_s, s1_s, s2_s, thr_s, lse_s = _outproj_peer(
        xs, att_s, gm_s, mods_s[2:5], ng, w_out_b, w_q_b, keys_b, pad_s, True, 1)
    peer_t_s = _experts(h2t_s, u_b, vt_b, s1_s, s2_s, thr_s, lse_s, pad_s)
    y_s = _final(x1_s, peer_t_s, mods_s[5], ng, pad_s, True, 1)

    y_prompt = y_p.reshape(batch, seq, D_MODEL)
    y_sample = y_s[:n_s].reshape(x_sample.shape)
    new_kv_p = []
    for kv, (win, _) in zip(kv_p, DIL_PAIRS):
        keep = min(win, seq)
        kv = kv.reshape(batch, seq, 2, ATT_HEADS, ATT_HD)[:, seq - keep:]
        new_kv_p.append(kv[None])
    new_kv_s = [kv[:n_s].reshape(1, n_s, 1, 2, ATT_HEADS, ATT_HD) for kv in (kv0_s, kv1_s, kv2_s)]
    state_gv = gvn_s[:n_s].reshape(1, n_s, 1, GM_HEADS, GM_HD)
    return (y_prompt, y_sample, *new_kv_p, *new_kv_s, state_gv)
```

```python
import functools
import math

import jax
import jax.numpy as jnp
from jax import lax
from jax.experimental import pallas as pl
from jax.experimental.pallas import tpu as pltpu

F32 = jnp.float32
BF16 = jnp.bfloat16

D_MODEL = 2048
ATT_HD = 64
ATT_HEADS = 8
GROUP_W = ATT_HEADS * ATT_HD
DIL_PAIRS = ((128, 1), (512, 4), (2048, 16))
N_DIL = 3
ATT_WIDTH = N_DIL * GROUP_W
QBLK = 128
BAND = 128
CHUNK = 128
GM_HEADS = 4
GM_HD = 128
GM_WIDTH = GM_HEADS * GM_HD
IN_WIDTH = 3 * ATT_WIDTH + 2 * GM_WIDTH
N_KEYS = 128
N_EXPERTS = N_KEYS * N_KEYS
PEER_HEADS = 8
PEER_HALF = 128
PEER_TOPK = 16
EPS = 1e-6
NEG_INF = float("-inf")
LOG2E = 1.4426950408889634

LANES = 128
VMEM_LIMIT = 56 * 1024 * 1024

Q_OFF = 0
KV_OFF = ATT_WIDTH
GU_OFF = ATT_WIDTH + N_DIL * 2 * GROUP_W
GV_OFF = GU_OFF + GM_WIDTH


def _cparams(sem, flags=None):
    return pltpu.CompilerParams(dimension_semantics=sem, vmem_limit_bytes=VMEM_LIMIT, flags=flags)


def _gelu(x):
    c = math.sqrt(2.0 / math.pi)
    return 0.5 * x * (1.0 + jnp.tanh(c * (x + 0.044715 * (x * x * x))))


def _gelu_sigmoid(x):
    k2 = -2.0 * math.sqrt(2.0 / math.pi) * LOG2E
    k1 = k2 * 0.044715
    return x / (1.0 + jnp.exp2(x * (k2 + k1 * (x * x))))


def _rms(x, g):
    return x * lax.rsqrt(jnp.mean(x * x, axis=-1, keepdims=True) + EPS) * g


def _dot(a, b):
    return jnp.dot(a, b, preferred_element_type=F32)


def _dot_nt(a, b):
    return lax.dot_general(a, b, (((1,), (1,)), ((), ())), preferred_element_type=F32)


def _ada_kernel(c_ref, w_ref, b_ref, o_ref):
    c = c_ref[...]
    act = (c / (1.0 + jnp.exp(-c))).astype(BF16)
    o_ref[...] = _dot(act, w_ref[...].astype(BF16)) + b_ref[...]


def _ada(c_all, w_ada, b_ada):
    rows = c_all.shape[0]
    n_out = w_ada.shape[1]
    tn = 1024
    return pl.pallas_call(
        _ada_kernel,
        grid=(n_out // tn,),
        in_specs=[
            pl.BlockSpec((rows, D_MODEL), lambda j: (0, 0)),
            pl.BlockSpec((D_MODEL, tn), lambda j: (0, j)),
            pl.BlockSpec((1, tn), lambda j: (0, j)),
        ],
        out_specs=pl.BlockSpec((rows, tn), lambda j: (0, j)),
        out_shape=jax.ShapeDtypeStruct((rows, n_out), F32),
        compiler_params=_cparams(("parallel",)),
        name="ada_mod",
    )(c_all, w_ada, b_ada)


def _modulated_norm(x_ref, sh_ref, sc_ref, g_ref):
    x = x_ref[...]
    return _rms(x, g_ref[...]) * (1.0 + sc_ref[...]) + sh_ref[...]


def _gm_layernorm(gv, lng_ref, lnb_ref, hh):
    v = gv[:, hh * GM_HD:(hh + 1) * GM_HD]
    mu = jnp.mean(v, axis=-1, keepdims=True)
    var = jnp.mean(jnp.square(v - mu), axis=-1, keepdims=True)
    return (v - mu) * lax.rsqrt(var + EPS) * lng_ref[pl.ds(hh, 1), :] + lnb_ref[pl.ds(hh, 1), :]


def _inproj_prompt_kernel(x_ref, sh_ref, sc_ref, g_ref, w_ref, ws_ref, bsb_ref, lng_ref, lnb_ref,
                          q0_ref, k0_ref, v0_ref, q1_ref, k1_ref, v1_ref, q2_ref, k2_ref, v2_ref,
                          kv0_ref, kv1_ref, kv2_ref, gm_ref, h_scr, y_scr):
    tm = x_ref.shape[0]
    h_scr[...] = _modulated_norm(x_ref, sh_ref, sc_ref, g_ref).astype(BF16)

    def proj(off):
        return _dot(h_scr[...], w_ref[:, off:off + GROUP_W])

    folded = ((q0_ref, k0_ref, v0_ref), (q1_ref, k1_ref, v1_ref), (q2_ref, k2_ref, v2_ref))
    kv_refs = (kv0_ref, kv1_ref, kv2_ref)
    for g, (_, dil) in enumerate(DIL_PAIRS):
        offs = (Q_OFF + g * GROUP_W, KV_OFF + g * 2 * GROUP_W, KV_OFF + g * 2 * GROUP_W + GROUP_W)
        for which, off in enumerate(offs):
            y = proj(off)
            if which > 0:
                kv_refs[g][:, (which - 1) * GROUP_W:which * GROUP_W] = y
            else:
                y = y * (ATT_HD ** -0.5)
            dst = folded[g][which]
            if dil == 1:
                dst[...] = y.astype(BF16)
            else:
                for cb in range(GROUP_W // LANES):
                    y_scr[cb] = y[:, cb * LANES:(cb + 1) * LANES]
                for r in range(dil):
                    for cb in range(GROUP_W // LANES):
                        dst[0, r, :, cb * LANES:(cb + 1) * LANES] = (
                            y_scr[cb, pl.ds(r, tm // dil, stride=dil), :].astype(BF16))

    gu = _gelu(proj(GU_OFF))
    gv = _gelu(proj(GV_OFF))
    row = lax.broadcasted_iota(jnp.int32, (CHUNK, CHUNK), 0)
    col = lax.broadcasted_iota(jnp.int32, (CHUNK, CHUNK), 1)
    for hh in range(GM_HEADS):
        gvn = _gm_layernorm(gv, lng_ref, lnb_ref, hh).astype(BF16)
        w_tril = jnp.where(row >= col, ws_ref[hh], 0.0).astype(BF16)
        for ci in range(tm // CHUNK):
            z = _dot(w_tril, gvn[ci * CHUNK:(ci + 1) * CHUNK]) + bsb_ref[hh]
            gu_blk = gu[ci * CHUNK:(ci + 1) * CHUNK, hh * GM_HD:(hh + 1) * GM_HD]
            gm_ref[ci * CHUNK:(ci + 1) * CHUNK, hh * GM_HD:(hh + 1) * GM_HD] = (gu_blk * z).astype(BF16)


def _inproj_prompt(x, mod, norm_g, w_in_p, w_s, bsb, ln_g, ln_b, batch, seq):
    n = x.shape[0]
    tm = 256
    tiles_per_seq = seq // tm

    def mod_spec(k):
        return pl.BlockSpec((None, 1, D_MODEL), lambda i: ((i // tiles_per_seq) * 6 + k, 0, 0))

    def folded_spec(dil):
        if dil == 1:
            return pl.BlockSpec((tm, GROUP_W), lambda i: (i, 0))
        return pl.BlockSpec((1, dil, tm // dil, GROUP_W),
                            lambda i: (i // tiles_per_seq, 0, i % tiles_per_seq, 0))

    def folded_shape(dil):
        if dil == 1:
            return jax.ShapeDtypeStruct((n, GROUP_W), BF16)
        return jax.ShapeDtypeStruct((batch, dil, seq // dil, GROUP_W), BF16)

    out_specs, out_shape = [], []
    for _, dil in DIL_PAIRS:
        for _ in range(3):
            out_specs.append(folded_spec(dil))
            out_shape.append(folded_shape(dil))
    for _ in range(N_DIL):
        out_specs.append(pl.BlockSpec((tm, 2 * GROUP_W), lambda i: (i, 0)))
        out_shape.append(jax.ShapeDtypeStruct((n, 2 * GROUP_W), F32))
    out_specs.append(pl.BlockSpec((tm, GM_WIDTH), lambda i: (i, 0)))
    out_shape.append(jax.ShapeDtypeStruct((n, GM_WIDTH), BF16))

    return pl.pallas_call(
        _inproj_prompt_kernel,
        grid=(n // tm,),
        in_specs=[
            pl.BlockSpec((tm, D_MODEL), lambda i: (i, 0)),
            mod_spec(0), mod_spec(1),
            pl.BlockSpec((None, 1, D_MODEL), lambda i: (0, 0, 0)),
            pl.BlockSpec((D_MODEL, IN_WIDTH), lambda i: (0, 0)),
            pl.BlockSpec((GM_HEADS, CHUNK, CHUNK), lambda i: (0, 0, 0)),
            pl.BlockSpec((GM_HEADS, CHUNK, GM_HD), lambda i: (0, 0, 0)),
            pl.BlockSpec((GM_HEADS, GM_HD), lambda i: (0, 0)),
            pl.BlockSpec((GM_HEADS, GM_HD), lambda i: (0, 0)),
        ],
        out_specs=out_specs,
        out_shape=out_shape,
        scratch_shapes=[pltpu.VMEM((tm, D_MODEL), BF16), pltpu.VMEM((GROUP_W // LANES, tm, LANES), F32)],
        compiler_params=_cparams(("parallel",)),
        name="inproj_prompt",
    )(x, mod, mod, norm_g, w_in_p, w_s, bsb, ln_g, ln_b)


def _inproj_sample_kernel(x_ref, sh_ref, sc_ref, g_ref, w_ref, wd_ref, b0_ref, lng_ref, lnb_ref,
                          q_ref, kv0_ref, kv1_ref, kv2_ref, gm_ref, gvn_ref):
    h = _modulated_norm(x_ref, sh_ref, sc_ref, g_ref).astype(BF16)
    q_ref[...] = _dot(h, w_ref[:, Q_OFF:Q_OFF + ATT_WIDTH])
    for g, kv_ref in enumerate((kv0_ref, kv1_ref, kv2_ref)):
        off = KV_OFF + g * 2 * GROUP_W
        kv_ref[...] = _dot(h, w_ref[:, off:off + 2 * GROUP_W])
    gu = _gelu(_dot(h, w_ref[:, GU_OFF:GU_OFF + GM_WIDTH]))
    gv = _gelu(_dot(h, w_ref[:, GV_OFF:GV_OFF + GM_WIDTH]))
    for hh in range(GM_HEADS):
        sl = slice(hh * GM_HD, (hh + 1) * GM_HD)
        gvn = _gm_layernorm(gv, lng_ref, lnb_ref, hh)
        gvn_ref[:, sl] = gvn
        z = wd_ref[:, sl] * gvn + b0_ref[:, sl]
        gm_ref[:, sl] = (gu[:, sl] * z).astype(BF16)


def _inproj_sample(x, sh, sc, norm_g, w_in_p, wd, b0, ln_g, ln_b):
    n = x.shape[0]
    full = lambda shape: pl.BlockSpec(shape, lambda i: tuple(0 for _ in shape))
    return pl.pallas_call(
        _inproj_sample_kernel,
        grid=(1,),
        in_specs=[
            full((n, D_MODEL)), full((n, D_MODEL)), full((n, D_MODEL)),
            pl.BlockSpec((None, 1, D_MODEL), lambda i: (0, 0, 0)),
            full((D_MODEL, IN_WIDTH)),
            full((1, GM_WIDTH)), full((1, GM_WIDTH)),
            full((GM_HEADS, GM_HD)), full((GM_HEADS, GM_HD)),
        ],
        out_specs=[full((n, ATT_WIDTH)), full((n, 2 * GROUP_W)), full((n, 2 * GROUP_W)),
                   full((n, 2 * GROUP_W)), full((n, GM_WIDTH)), full((n, GM_WIDTH))],
        out_shape=[jax.ShapeDtypeStruct((n, ATT_WIDTH), F32)]
        + [jax.ShapeDtypeStruct((n, 2 * GROUP_W), F32)] * 3
        + [jax.ShapeDtypeStruct((n, GM_WIDTH), BF16), jax.ShapeDtypeStruct((n, GM_WIDTH), F32)],
        compiler_params=_cparams(("arbitrary",)),
        name="inproj_sample",
    )(x, sh, sc, norm_g, w_in_p, wd, b0, ln_g, ln_b)


def _attn_block(qt, kt, vt, off, nkeys):
    lane = lax.broadcasted_iota(jnp.int32, (1, LANES), 1)
    lo = lane < ATT_HD
    qi = lax.broadcasted_iota(jnp.int32, (QBLK, nkeys), 0)
    kk = lax.broadcasted_iota(jnp.int32, (QBLK, nkeys), 1)
    dist = qi - kk + off
    valid = lax.bitcast_convert_type(dist, jnp.uint32) <= jnp.uint32(BAND)
    res = []
    for sel in (lo, jnp.logical_not(lo)):
        qm = jnp.where(sel, qt, jnp.zeros_like(qt))
        s = jnp.where(valid, _dot_nt(qm, kt), NEG_INF)
        m = jnp.max(s, axis=-1, keepdims=True)
        p = jnp.exp(s - m)
        den = jnp.sum(p, axis=-1, keepdims=True)
        o = _dot(p.astype(BF16), vt) * (1.0 / den)
        res.append((o, m + jnp.log(den)))
    o = jnp.where(lo, res[0][0], res[1][0])
    lse = jnp.where(lo, res[0][1], res[1][1])
    return o, lse


def _attn_prompt_kernel(q0_ref, k0_ref, v0_ref, q1_ref, k1_ref, v1_ref, q2_ref, k2_ref, v2_ref,
                        a0_ref, a1_ref, a2_ref, of_scr, lf_scr, to_scr, tl_scr):
    seq = q0_ref.shape[0]

    def run_block(g, qt, kt, vt, off, nkeys, row0):
        o, lse = _attn_block(qt, kt, vt, off, nkeys)
        of_scr[g, pl.ds(row0, QBLK), :] = o
        lf_scr[g, pl.ds(row0, QBLK), :] = lse

    par = 4

    def g0_body(it, carry):
        for u in range(par):
            qb = it * par + u
            qs = pl.multiple_of(qb * QBLK, QBLK)
            ks = pl.multiple_of(jnp.maximum(qb - 1, 0) * QBLK, QBLK)
            run_block(0, q0_ref[pl.ds(qs, QBLK), :], k0_ref[pl.ds(ks, 2 * QBLK), :],
                      v0_ref[pl.ds(ks, 2 * QBLK), :], qs - ks, 2 * QBLK, qs)
        return carry

    lax.fori_loop(0, seq // QBLK // par, g0_body, 0)

    dil1 = DIL_PAIRS[1][1]
    len1 = seq // dil1

    def g1_body(r, carry):
        for qb in range(len1 // QBLK):
            qs = qb * QBLK
            ks = max(qb - 1, 0) * QBLK
            run_block(1, q1_ref[0, r, pl.ds(qs, QBLK), :], k1_ref[0, r, pl.ds(ks, 2 * QBLK), :],
                      v1_ref[0, r, pl.ds(ks, 2 * QBLK), :], qs - ks, 2 * QBLK,
                      pl.multiple_of(r * len1 + qs, QBLK))
        return carry

    lax.fori_loop(0, dil1, g1_body, 0)

    dil2 = DIL_PAIRS[2][1]
    len2 = seq // dil2

    def g2_body(it, carry):
        for u in range(par):
            r = it * par + u
            run_block(2, q2_ref[0, r], k2_ref[0, r], v2_ref[0, r], 0, QBLK,
                      pl.multiple_of(r * len2, QBLK))
        return carry

    lax.fori_loop(0, dil2 // par, g2_body, 0)

    def mix_body(nt, carry):
        t0 = pl.multiple_of(nt * QBLK, QBLK)
        o_nat = [of_scr[0, pl.ds(t0, QBLK), :]]
        l_nat = [lf_scr[0, pl.ds(t0, QBLK), :]]
        for g, dil, length in ((1, dil1, len1), (2, dil2, len2)):
            per = QBLK // dil
            for r in range(dil):
                src = pl.multiple_of(r * length + nt * per, per)
                to_scr[g - 1, pl.ds(r, per, stride=dil), :] = of_scr[g, pl.ds(src, per), :]
                tl_scr[g - 1, pl.ds(r, per, stride=dil), :] = lf_scr[g, pl.ds(src, per), :]
            o_nat.append(to_scr[g - 1])
            l_nat.append(tl_scr[g - 1])
        mx = jnp.maximum(jnp.maximum(l_nat[0], l_nat[1]), l_nat[2])
        e = [jnp.exp(l - mx) for l in l_nat]
        inv = 1.0 / (e[0] + e[1] + e[2])
        for g, a_ref in enumerate((a0_ref, a1_ref, a2_ref)):
            a_ref[pl.ds(t0, QBLK), :] = (o_nat[g] * (e[g] * inv)).astype(BF16)
        return carry

    lax.fori_loop(0, seq // QBLK, mix_body, 0)


def _attn_prompt(folded, batch, seq):
    n = batch * seq
    pairs = GROUP_W // LANES
    in_specs = []
    for _, dil in DIL_PAIRS:
        for _ in range(3):
            if dil == 1:
                in_specs.append(pl.BlockSpec((seq, LANES), lambda b, hp: (b, hp)))
            else:
                in_specs.append(pl.BlockSpec((1, dil, seq // dil, LANES), lambda b, hp: (b, 0, 0, hp)))
    return pl.pallas_call(
        _attn_prompt_kernel,
        grid=(batch, pairs),
        in_specs=in_specs,
        out_specs=[pl.BlockSpec((seq, LANES), lambda b, hp: (b, hp))] * N_DIL,
        out_shape=[jax.ShapeDtypeStruct((n, GROUP_W), BF16)] * N_DIL,
        scratch_shapes=[pltpu.VMEM((N_DIL, seq, LANES), F32), pltpu.VMEM((N_DIL, seq, LANES), F32),
                        pltpu.VMEM((N_DIL - 1, QBLK, LANES), F32), pltpu.VMEM((N_DIL - 1, QBLK, LANES), F32)],
        compiler_params=_cparams(("parallel", "parallel")),
        name="attn_prompt",
    )(*folded)


def _attn_sample_kernel(q_ref, n0_ref, n1_ref, n2_ref, c0_ref, c1_ref, c2_ref, o_ref):
    outs, lses = [], []
    for g, (n_ref, c_ref) in enumerate(((n0_ref, c0_ref), (n1_ref, c1_ref), (n2_ref, c2_ref))):
        q = q_ref[0, g] * (ATT_HD ** -0.5)
        k_buf, v_buf = c_ref[0, :, 0], c_ref[0, :, 1]
        k_new, v_new = n_ref[0, 0], n_ref[0, 1]
        s_buf = jnp.sum(k_buf * q[None], axis=-1, keepdims=True)
        s_new = jnp.sum(k_new * q, axis=-1, keepdims=True)
        m = jnp.maximum(jnp.max(s_buf, axis=0), s_new)
        p_buf = jnp.exp(s_buf - m[None])
        p_new = jnp.exp(s_new - m)
        den = jnp.sum(p_buf, axis=0) + p_new
        outs.append((jnp.sum(p_buf * v_buf, axis=0) + p_new * v_new) / den)
        lses.append(m + jnp.log(den))
    mx = jnp.maximum(jnp.maximum(lses[0], lses[1]), lses[2])
    e = [jnp.exp(l - mx) for l in lses]
    inv = 1.0 / (e[0] + e[1] + e[2])
    for g in range(N_DIL):
        o_ref[0, g] = outs[g] * (e[g] * inv)


def _attn_sample(q, new_kvs, caches):
    n_real = q.shape[0]
    cviews = [c.reshape(n_real, c.shape[1] // dil, dil, 2, ATT_HEADS, ATT_HD)
              for c, (_, dil) in zip(caches, DIL_PAIRS)]
    tile = (ATT_HEADS, ATT_HD)
    return pl.pallas_call(
        _attn_sample_kernel,
        grid=(n_real,),
        in_specs=[pl.BlockSpec((1, N_DIL) + tile, lambda b: (b, 0, 0, 0))]
        + [pl.BlockSpec((1, 2) + tile, lambda b: (b, 0, 0, 0))] * N_DIL
        + [pl.BlockSpec((1, BAND, None, 2) + tile, lambda b: (b, 0, 0, 0, 0, 0))] * N_DIL,
        out_specs=pl.BlockSpec((1, N_DIL) + tile, lambda b: (b, 0, 0, 0)),
        out_shape=jax.ShapeDtypeStruct((n_real, N_DIL) + tile, F32),
        compiler_params=_cparams(("arbitrary",)),
        name="attn_sample",
    )(q, *new_kvs, *cviews)


def _topk_desc(sc, k):
    vals = []
    cur = sc
    for i in range(k):
        m = jnp.max(cur, axis=0, keepdims=True)
        vals.append(m)
        if i + 1 < k:
            cur = jnp.where(cur == m, NEG_INF, cur)
    return vals


def _outproj_peer_kernel(x_ref, a0_ref, a1_ref, a2_ref, gm_ref, gt1_ref, sh2_ref, sc2_ref, g1_ref, g2_ref,
                         wo_ref, wq_ref, keys_ref,
                         x1_ref, h2t_ref, s1_ref, s2_ref, e2_ref, thr_ref, lse_ref):
    mo = _dot(a0_ref[...], wo_ref[0:GROUP_W, :])
    mo += _dot(a1_ref[...], wo_ref[GROUP_W:2 * GROUP_W, :])
    mo += _dot(a2_ref[...], wo_ref[2 * GROUP_W:3 * GROUP_W, :])
    mo += _dot(gm_ref[...], wo_ref[3 * GROUP_W:4 * GROUP_W, :])
    x1 = x_ref[...] + gt1_ref[...] * _rms(mo, g1_ref[...])
    x1_ref[...] = x1
    h2 = _rms(x1, g2_ref[...]) * (1.0 + sc2_ref[...]) + sh2_ref[...]
    h2t_ref[...] = jnp.transpose(h2).astype(BF16)
    qr = _dot(h2.astype(BF16), wq_ref[...])
    lane_groups = thr_ref.shape[0]
    for h in range(PEER_HEADS):
        tops = []
        for p, s_ref in enumerate((s1_ref, s2_ref)):
            c0 = (h * 2 + p) * PEER_HALF
            sc = _dot_nt(keys_ref[p], qr[:, c0:c0 + PEER_HALF].astype(BF16)) * LOG2E
            for lg in range(lane_groups):
                s_ref[lg, h] = sc[:, lg * LANES:(lg + 1) * LANES]
            tops.append(_topk_desc(sc, PEER_TOPK))
            if p == 1:
                e2 = jnp.exp2(sc - tops[1][0])
                for lg in range(lane_groups):
                    e2_ref[lg, h] = e2[:, lg * LANES:(lg + 1) * LANES]
        t1 = jnp.concatenate(tops[0], axis=0)
        cand = jnp.concatenate([t1 + tops[1][b] for b in range(PEER_TOPK)], axis=0)
        best = _topk_desc(cand, PEER_TOPK)
        z = jnp.ones_like(best[0])
        for v in best[1:]:
            z = z + jnp.exp2(v - best[0])
        lse = best[0] + jnp.log(z) * LOG2E - tops[1][0]
        for lg in range(lane_groups):
            thr_ref[lg, pl.ds(h, 1), :] = best[-1][:, lg * LANES:(lg + 1) * LANES]
            lse_ref[lg, pl.ds(h, 1), :] = lse[:, lg * LANES:(lg + 1) * LANES]


def _outproj_peer(x, atts, gm, mods, norm_g, w_out, w_q, keys, tm, per_row_mod, tiles_per_seq):
    n = x.shape[0]
    if per_row_mod:
        mod_specs = [pl.BlockSpec((tm, D_MODEL), lambda i: (i, 0))] * 3
    else:
        mod_specs = [pl.BlockSpec((None, 1, D_MODEL), lambda i, k=k: ((i // tiles_per_seq) * 6 + k, 0, 0))
                     for k in (2, 3, 4)]
    row_spec = lambda w: pl.BlockSpec((tm, w), lambda i: (i, 0))
    const = lambda shape: pl.BlockSpec(shape, lambda i: tuple(0 for _ in shape))
    return pl.pallas_call(
        _outproj_peer_kernel,
        grid=(n // tm,),
        in_specs=[row_spec(D_MODEL)] + [row_spec(GROUP_W)] * 4 + mod_specs
        + [pl.BlockSpec((None, 1, D_MODEL), lambda i: (1, 0, 0)),
           pl.BlockSpec((None, 1, D_MODEL), lambda i: (2, 0, 0)),
           const((D_MODEL, D_MODEL)), const((D_MODEL, D_MODEL)), const((2, N_KEYS, PEER_HALF))],
        out_specs=[row_spec(D_MODEL),
                   pl.BlockSpec((D_MODEL, tm), lambda i: (0, i)),
                   pl.BlockSpec((tm // LANES, PEER_HEADS, N_KEYS, LANES), lambda i: (i, 0, 0, 0)),
                   pl.BlockSpec((tm // LANES, PEER_HEADS, N_KEYS, LANES), lambda i: (i, 0, 0, 0)),
                   pl.BlockSpec((tm // LANES, PEER_HEADS, N_KEYS, LANES), lambda i: (i, 0, 0, 0)),
                   pl.BlockSpec((tm // LANES, PEER_HEADS, LANES), lambda i: (i, 0, 0)),
                   pl.BlockSpec((tm // LANES, PEER_HEADS, LANES), lambda i: (i, 0, 0))],
        out_shape=[jax.ShapeDtypeStruct((n, D_MODEL), F32),
                   jax.ShapeDtypeStruct((D_MODEL, n), BF16),
                   jax.ShapeDtypeStruct((n // LANES, PEER_HEADS, N_KEYS, LANES), F32),
                   jax.ShapeDtypeStruct((n // LANES, PEER_HEADS, N_KEYS, LANES), F32),
                   jax.ShapeDtypeStruct((n // LANES, PEER_HEADS, N_KEYS, LANES), F32),
                   jax.ShapeDtypeStruct((n // LANES, PEER_HEADS, LANES), F32),
                   jax.ShapeDtypeStruct((n // LANES, PEER_HEADS, LANES), F32)],
        compiler_params=_cparams(("parallel",)),
        name="outproj_peer",
    )(x, *atts, gm, *mods, norm_g, norm_g, w_out, w_q, keys)


def _expert_kernel(h2t_ref, u_ref, vt_ref, s1_ref, s2_ref, e2_ref, thr_ref, lse_ref, o_ref,
                   acc_scr, g_scr, w_scr, rowb_scr, thrb_scr, tile_scr):
    j = pl.program_id(1)
    n_tiles = pl.num_programs(1) - 1
    te = u_ref.shape[0]
    rows_per_tile = te // N_KEYS
    lane_groups = s1_ref.shape[0]
    n_rows = rows_per_tile * PEER_HEADS
    ch = 8

    @pl.when(j == 0)
    def _():
        acc_scr[...] = jnp.zeros_like(acc_scr)
        g_scr[...] = jnp.zeros_like(g_scr)
        for lg in range(lane_groups):
            for h in range(PEER_HEADS):
                thrb_scr[lg, h] = jnp.broadcast_to(thr_ref[lg, pl.ds(h, 1), :], (ch, LANES))

    jt = jnp.minimum(j, n_tiles - 1)
    for lg in range(lane_groups):
        for rr in range(rows_per_tile):
            tile = tile_scr.at[lg * rows_per_tile + rr]
            for h in range(PEER_HEADS):
                tile[pl.ds(h, 1), :] = s1_ref[lg, h, pl.ds(jt * rows_per_tile + rr, 1), :]
            rows8 = tile[...]
            e8 = jnp.exp2(rows8 - lse_ref[lg])
            for h in range(PEER_HEADS):
                rowb_scr[lg, rr * PEER_HEADS + h] = jnp.broadcast_to(rows8[h:h + 1], (ch, LANES))
                rowb_scr[lg, n_rows + rr * PEER_HEADS + h] = jnp.broadcast_to(e8[h:h + 1], (ch, LANES))

    def build_gates(lg):
        for cc in range(N_KEYS // ch):
            gates = [None] * rows_per_tile
            for h in range(PEER_HEADS):
                b = s2_ref[lg, h, cc * ch:(cc + 1) * ch, :]
                eb = e2_ref[lg, h, cc * ch:(cc + 1) * ch, :]
                t = thrb_scr[lg, h]
                for rr in range(rows_per_tile):
                    a = rowb_scr[lg, rr * PEER_HEADS + h]
                    ea = rowb_scr[lg, n_rows + rr * PEER_HEADS + h]
                    term = jnp.where(a + b >= t, ea * eb, 0.0)
                    gates[rr] = term if gates[rr] is None else gates[rr] + term
            for rr in range(rows_per_tile):
                g_scr[lg, rr * N_KEYS + cc * ch:rr * N_KEYS + (cc + 1) * ch, :] = gates[rr]

    act = _dot(u_ref[...], h2t_ref[...])
    for lg in range(lane_groups):
        lanes = slice(lg * LANES, (lg + 1) * LANES)
        w_scr[:, lanes] = (g_scr[lg] * _gelu_sigmoid(act[:, lanes])).astype(BF16)

    d_slab = acc_scr.shape[0] // lane_groups

    def piece(lg, carry):
        d0 = pl.multiple_of(lg * d_slab, d_slab)
        acc_scr[pl.ds(d0, d_slab), :] += _dot(vt_ref[pl.ds(d0, d_slab), :], w_scr[...])
        build_gates(lg)
        return carry

    lax.fori_loop(0, lane_groups, piece, 0)


    @pl.when(j == n_tiles)
    def _():
        o_ref[...] = acc_scr[...]


def _experts(h2t, u_bf, vt_bf, s1, s2, e2, thr, lse, tm):
    n = h2t.shape[1]
    te = 512
    n_tiles = N_EXPERTS // te
    prev_tile = lambda j: jnp.maximum(j - 1, 0)
    return pl.pallas_call(
        _expert_kernel,
        grid=(n // tm, n_tiles + 1),
        in_specs=[
            pl.BlockSpec((D_MODEL, tm), lambda i, j: (0, i)),
            pl.BlockSpec((te, D_MODEL), lambda i, j: (prev_tile(j), 0)),
            pl.BlockSpec((D_MODEL, te), lambda i, j: (0, prev_tile(j))),
            pl.BlockSpec((tm // LANES, PEER_HEADS, N_KEYS, LANES), lambda i, j: (i, 0, 0, 0)),
            pl.BlockSpec((tm // LANES, PEER_HEADS, N_KEYS, LANES), lambda i, j: (i, 0, 0, 0)),
            pl.BlockSpec((tm // LANES, PEER_HEADS, N_KEYS, LANES), lambda i, j: (i, 0, 0, 0)),
            pl.BlockSpec((tm // LANES, PEER_HEADS, LANES), lambda i, j: (i, 0, 0)),
            pl.BlockSpec((tm // LANES, PEER_HEADS, LANES), lambda i, j: (i, 0, 0)),
        ],
        out_specs=pl.BlockSpec((D_MODEL, tm), lambda i, j: (0, i)),
        out_shape=jax.ShapeDtypeStruct((D_MODEL, n), F32),
        scratch_shapes=[pltpu.VMEM((D_MODEL, tm), F32), pltpu.VMEM((tm // LANES, te, LANES), F32),
                        pltpu.VMEM((te, tm), BF16),
                        pltpu.VMEM((tm // LANES, 2 * (te // N_KEYS) * PEER_HEADS, 8, LANES), F32),
                        pltpu.VMEM((tm // LANES, PEER_HEADS, 8, LANES), F32),
                        pltpu.VMEM((tm // LANES * (te // N_KEYS), PEER_HEADS, LANES), F32)],
        compiler_params=_cparams(("parallel", "arbitrary")),
        name="peer_experts",
    )(h2t, u_bf, vt_bf, s1, s2, e2, thr, lse)


def _final_kernel(x1_ref, pt_ref, gt2_ref, g3_ref, y_ref):
    peer = jnp.transpose(pt_ref[...])
    y_ref[...] = x1_ref[...] + gt2_ref[...] * _rms(peer, g3_ref[...])


def _final(x1, peer_t, gt2, norm_g, tm, per_row_mod, tiles_per_seq):
    n = x1.shape[0]
    if per_row_mod:
        gt2_spec = pl.BlockSpec((tm, D_MODEL), lambda i: (i, 0))
    else:
        gt2_spec = pl.BlockSpec((None, 1, D_MODEL), lambda i: ((i // tiles_per_seq) * 6 + 5, 0, 0))
    return pl.pallas_call(
        _final_kernel,
        grid=(n // tm,),
        in_specs=[pl.BlockSpec((tm, D_MODEL), lambda i: (i, 0)),
                  pl.BlockSpec((D_MODEL, tm), lambda i: (0, i)),
                  gt2_spec,
                  pl.BlockSpec((None, 1, D_MODEL), lambda i: (3, 0, 0))],
        out_specs=pl.BlockSpec((tm, D_MODEL), lambda i: (i, 0)),
        out_shape=jax.ShapeDtypeStruct((n, D_MODEL), F32),
        compiler_params=_cparams(("parallel",)),
        name="final_residual",
    )(x1, peer_t, gt2, norm_g)


def _permute_w_in(w_in):
    q = w_in[:, :ATT_WIDTH]
    k = w_in[:, ATT_WIDTH:2 * ATT_WIDTH]
    v = w_in[:, 2 * ATT_WIDTH:3 * ATT_WIDTH]
    parts = [q]
    for g in range(N_DIL):
        parts += [k[:, g * GROUP_W:(g + 1) * GROUP_W], v[:, g * GROUP_W:(g + 1) * GROUP_W]]
    parts.append(w_in[:, 3 * ATT_WIDTH:])
    return jnp.concatenate(parts, axis=1).astype(BF16)


def kernel(x_prompt, x_sample, c_prompt, c_sample, cache_kv_dil1, cache_kv_dil4, cache_kv_dil16, w_ada, b_ada, norm_g, w_in, gm_ln_g, gm_ln_b, w_s, b_s, w_out, w_q, sub_keys, expert_u, expert_v):
    batch, seq, _ = x_prompt.shape
    n_s = x_sample.shape[0] * x_sample.shape[1]
    assert w_ada.shape[0] == 1 and x_sample.shape[1] == 1
    n_p = batch * seq
    pad_s = 128

    w_in_p = _permute_w_in(w_in[0])
    w_out_b = w_out[0].astype(BF16)
    w_q_b = w_q[0].astype(BF16)
    keys_b = sub_keys[0].astype(BF16)
    u_b = expert_u[0].astype(BF16)
    vt_b = jnp.transpose(expert_v[0]).astype(BF16)
    ng = norm_g[0].reshape(4, 1, D_MODEL)
    ln_g, ln_b = gm_ln_g[0], gm_ln_b[0]

    rows = batch + n_s
    c_all = jnp.zeros((16, D_MODEL), F32).at[:rows].set(jnp.concatenate([c_prompt, c_sample], axis=0))
    mod = _ada(c_all, w_ada[0], b_ada)
    mod_p = mod.reshape(16 * 6, 1, D_MODEL)
    mod_s = jnp.zeros((pad_s, 6, D_MODEL), F32).at[:n_s].set(mod[batch:rows].reshape(n_s, 6, D_MODEL))
    mods_s = [mod_s[:, k] for k in range(6)]

    xp = x_prompt.reshape(n_p, D_MODEL)
    bsb = jnp.broadcast_to(b_s[0][:, :, None], (GM_HEADS, CHUNK, GM_HD))
    outs = _inproj_prompt(xp, mod_p, ng, w_in_p, w_s[0], bsb, ln_g, ln_b, batch, seq)
    folded, kv_p, gm_p = outs[:9], outs[9:12], outs[12]
    att_p = _attn_prompt(folded, batch, seq)
    tm_p = 256
    x1_p, h2t_p, *sel_p = _outproj_peer(
        xp, att_p, gm_p, [mod_p] * 3, ng, w_out_b, w_q_b, keys_b, tm_p, False, seq // tm_p)
    peer_t_p = _experts(h2t_p, u_b, vt_b, *sel_p, 512)
    y_p = _final(x1_p, peer_t_p, mod_p, ng, tm_p, False, seq // tm_p)

    xs = jnp.zeros((pad_s, D_MODEL), F32).at[:n_s].set(x_sample.reshape(n_s, D_MODEL))
    wd = jnp.repeat(w_s[0][:, 0, 0], GM_HD)[None, :]
    b0 = jnp.repeat(b_s[0][:, 0], GM_HD)[None, :]
    q_s, kv0_s, kv1_s, kv2_s, gm_s, gvn_s = _inproj_sample(
        xs, mods_s[0], mods_s[1], ng, w_in_p, wd, b0, ln_g, ln_b)
    new_kv_s = [kv[:n_s].reshape(n_s, 2, ATT_HEADS, ATT_HD) for kv in (kv0_s, kv1_s, kv2_s)]
    att_s = _attn_sample(q_s[:n_s].reshape(n_s, N_DIL, ATT_HEADS, ATT_HD), new_kv_s,
                         (cache_kv_dil1[0], cache_kv_dil4[0], cache_kv_dil16[0]))
    att_s = att_s.reshape(n_s, N_DIL, GROUP_W).astype(BF16)
    att_s = [jnp.zeros((pad_s, GROUP_W), BF16).at[:n_s].set(att_s[:, g]) for g in range(N_DIL)]
    x1_s, h2t_s, *sel_s = _outproj_peer(
        xs, att_s, gm_s, mods_s[2:5], ng, w_out_b, w_q_b, keys_b, pad_s, True, 1)
    peer_t_s = _experts(h2t_s, u_b, vt_b, *sel_s, pad_s)
    y_s = _final(x1_s, peer_t_s, mods_s[5], ng, pad_s, True, 1)

    y_prompt = y_p.reshape(batch, seq, D_MODEL)
    y_sample = y_s[:n_s].reshape(x_sample.shape)
    new_kv_p = []
    for kv, (win, _) in zip(kv_p, DIL_PAIRS):
        keep = min(win, seq)
        kv = kv.reshape(batch, seq, 2, ATT_HEADS, ATT_HD)[:, seq - keep:]
        new_kv_p.append(kv[None])
    new_kv_s = [kv.reshape(1, n_s, 1, 2, ATT_HEADS, ATT_HD) for kv in new_kv_s]
    state_gv = gvn_s[:n_s].reshape(1, n_s, 1, GM_HEADS, GM_HD)
    return (y_prompt, y_sample, *new_kv_p, *new_kv_s, state_gv)
```

```python
import functools
import math

import jax
import jax.numpy as jnp
from jax import lax
from jax.experimental import pallas as pl
from jax.experimental.pallas import tpu as pltpu

F32 = jnp.float32
BF16 = jnp.bfloat16

D_MODEL = 2048
ATT_HD = 64
ATT_HEADS = 8
GROUP_W = ATT_HEADS * ATT_HD
DIL_PAIRS = ((128, 1), (512, 4), (2048, 16))
N_DIL = 3
ATT_WIDTH = N_DIL * GROUP_W
QBLK = 128
BAND = 128
CHUNK = 128
GM_HEADS = 4
GM_HD = 128
GM_WIDTH = GM_HEADS * GM_HD
IN_WIDTH = 3 * ATT_WIDTH + 2 * GM_WIDTH
N_KEYS = 128
N_EXPERTS = N_KEYS * N_KEYS
PEER_HEADS = 8
PEER_HALF = 128
PEER_TOPK = 16
EPS = 1e-6
NEG_INF = float("-inf")
LOG2E = 1.4426950408889634

LANES = 128
VMEM_LIMIT = 56 * 1024 * 1024

Q_OFF = 0
KV_OFF = ATT_WIDTH
GU_OFF = ATT_WIDTH + N_DIL * 2 * GROUP_W
GV_OFF = GU_OFF + GM_WIDTH


def _cparams(sem, flags=None):
    return pltpu.CompilerParams(dimension_semantics=sem, vmem_limit_bytes=VMEM_LIMIT, flags=flags)


def _gelu(x):
    c = math.sqrt(2.0 / math.pi)
    return 0.5 * x * (1.0 + jnp.tanh(c * (x + 0.044715 * (x * x * x))))


def _gelu_sigmoid(x):
    k2 = -2.0 * math.sqrt(2.0 / math.pi) * LOG2E
    k1 = k2 * 0.044715
    return x / (1.0 + jnp.exp2(x * (k2 + k1 * (x * x))))


def _rms(x, g):
    return x * lax.rsqrt(jnp.mean(x * x, axis=-1, keepdims=True) + EPS) * g


def _dot(a, b):
    return jnp.dot(a, b, preferred_element_type=F32)


def _dot_nt(a, b):
    return lax.dot_general(a, b, (((1,), (1,)), ((), ())), preferred_element_type=F32)


def _ada_kernel(c_ref, w_ref, b_ref, o_ref):
    c = c_ref[...]
    act = (c / (1.0 + jnp.exp(-c))).astype(BF16)
    o_ref[...] = _dot(act, w_ref[...].astype(BF16)) + b_ref[...]


def _ada(c_all, w_ada, b_ada):
    rows = c_all.shape[0]
    n_out = w_ada.shape[1]
    tn = 1024
    return pl.pallas_call(
        _ada_kernel,
        grid=(n_out // tn,),
        in_specs=[
            pl.BlockSpec((rows, D_MODEL), lambda j: (0, 0)),
            pl.BlockSpec((D_MODEL, tn), lambda j: (0, j)),
            pl.BlockSpec((1, tn), lambda j: (0, j)),
        ],
        out_specs=pl.BlockSpec((rows, tn), lambda j: (0, j)),
        out_shape=jax.ShapeDtypeStruct((rows, n_out), F32),
        compiler_params=_cparams(("parallel",)),
        name="ada_mod",
    )(c_all, w_ada, b_ada)


def _modulated_norm(x_ref, sh_ref, sc_ref, g_ref):
    x = x_ref[...]
    return _rms(x, g_ref[...]) * (1.0 + sc_ref[...]) + sh_ref[...]


def _gm_layernorm(gv, lng_ref, lnb_ref, hh):
    v = gv[:, hh * GM_HD:(hh + 1) * GM_HD]
    mu = jnp.mean(v, axis=-1, keepdims=True)
    var = jnp.mean(jnp.square(v - mu), axis=-1, keepdims=True)
    return (v - mu) * lax.rsqrt(var + EPS) * lng_ref[pl.ds(hh, 1), :] + lnb_ref[pl.ds(hh, 1), :]


def _inproj_prompt_kernel(x_ref, sh_ref, sc_ref, g_ref, w_ref, ws_ref, bsb_ref, lng_ref, lnb_ref,
                          q0_ref, k0_ref, v0_ref, q1_ref, k1_ref, v1_ref, q2_ref, k2_ref, v2_ref,
                          kv0_ref, kv1_ref, kv2_ref, gm_ref, h_scr, y_scr):
    tm = x_ref.shape[0]
    h_scr[...] = _modulated_norm(x_ref, sh_ref, sc_ref, g_ref).astype(BF16)

    def proj(off):
        return _dot(h_scr[...], w_ref[:, off:off + GROUP_W])

    folded = ((q0_ref, k0_ref, v0_ref), (q1_ref, k1_ref, v1_ref), (q2_ref, k2_ref, v2_ref))
    kv_refs = (kv0_ref, kv1_ref, kv2_ref)
    for g, (_, dil) in enumerate(DIL_PAIRS):
        offs = (Q_OFF + g * GROUP_W, KV_OFF + g * 2 * GROUP_W, KV_OFF + g * 2 * GROUP_W + GROUP_W)
        for which, off in enumerate(offs):
            y = proj(off)
            if which > 0:
                kv_refs[g][0, (which - 1) * GROUP_W:which * GROUP_W, :] = jnp.transpose(y)
            else:
                y = y * (ATT_HD ** -0.5)
            dst = folded[g][which]
            if dil == 1:
                dst[...] = y.astype(BF16)
            else:
                for cb in range(GROUP_W // LANES):
                    y_scr[cb] = y[:, cb * LANES:(cb + 1) * LANES]
                for r in range(dil):
                    for cb in range(GROUP_W // LANES):
                        dst[0, r, :, cb * LANES:(cb + 1) * LANES] = (
                            y_scr[cb, pl.ds(r, tm // dil, stride=dil), :].astype(BF16))

    gu = _gelu(proj(GU_OFF))
    gv = _gelu(proj(GV_OFF))
    row = lax.broadcasted_iota(jnp.int32, (CHUNK, CHUNK), 0)
    col = lax.broadcasted_iota(jnp.int32, (CHUNK, CHUNK), 1)
    for hh in range(GM_HEADS):
        gvn = _gm_layernorm(gv, lng_ref, lnb_ref, hh).astype(BF16)
        w_tril = jnp.where(row >= col, ws_ref[hh], 0.0).astype(BF16)
        for ci in range(tm // CHUNK):
            z = _dot(w_tril, gvn[ci * CHUNK:(ci + 1) * CHUNK]) + bsb_ref[hh]
            gu_blk = gu[ci * CHUNK:(ci + 1) * CHUNK, hh * GM_HD:(hh + 1) * GM_HD]
            gm_ref[ci * CHUNK:(ci + 1) * CHUNK, hh * GM_HD:(hh + 1) * GM_HD] = (gu_blk * z).astype(BF16)


def _inproj_prompt(x, mod, norm_g, w_in_p, w_s, bsb, ln_g, ln_b, batch, seq):
    n = x.shape[0]
    tm = 256
    tiles_per_seq = seq // tm

    def mod_spec(k):
        return pl.BlockSpec((None, 1, D_MODEL), lambda i: ((i // tiles_per_seq) * 6 + k, 0, 0))

    def folded_spec(dil):
        if dil == 1:
            return pl.BlockSpec((tm, GROUP_W), lambda i: (i, 0))
        return pl.BlockSpec((1, dil, tm // dil, GROUP_W),
                            lambda i: (i // tiles_per_seq, 0, i % tiles_per_seq, 0))

    def folded_shape(dil):
        if dil == 1:
            return jax.ShapeDtypeStruct((n, GROUP_W), BF16)
        return jax.ShapeDtypeStruct((batch, dil, seq // dil, GROUP_W), BF16)

    out_specs, out_shape = [], []
    for _, dil in DIL_PAIRS:
        for _ in range(3):
            out_specs.append(folded_spec(dil))
            out_shape.append(folded_shape(dil))
    for _ in range(N_DIL):
        out_specs.append(pl.BlockSpec((1, 2 * GROUP_W, tm),
                                      lambda i: (i // tiles_per_seq, 0, i % tiles_per_seq)))
        out_shape.append(jax.ShapeDtypeStruct((batch, 2 * GROUP_W, seq), F32))
    out_specs.append(pl.BlockSpec((tm, GM_WIDTH), lambda i: (i, 0)))
    out_shape.append(jax.ShapeDtypeStruct((n, GM_WIDTH), BF16))

    return pl.pallas_call(
        _inproj_prompt_kernel,
        grid=(n // tm,),
        in_specs=[
            pl.BlockSpec((tm, D_MODEL), lambda i: (i, 0)),
            mod_spec(0), mod_spec(1),
            pl.BlockSpec((None, 1, D_MODEL), lambda i: (0, 0, 0)),
            pl.BlockSpec((D_MODEL, IN_WIDTH), lambda i: (0, 0)),
            pl.BlockSpec((GM_HEADS, CHUNK, CHUNK), lambda i: (0, 0, 0)),
            pl.BlockSpec((GM_HEADS, CHUNK, GM_HD), lambda i: (0, 0, 0)),
            pl.BlockSpec((GM_HEADS, GM_HD), lambda i: (0, 0)),
            pl.BlockSpec((GM_HEADS, GM_HD), lambda i: (0, 0)),
        ],
        out_specs=out_specs,
        out_shape=out_shape,
        scratch_shapes=[pltpu.VMEM((tm, D_MODEL), BF16), pltpu.VMEM((GROUP_W // LANES, tm, LANES), F32)],
        compiler_params=_cparams(("parallel",)),
        name="inproj_prompt",
    )(x, mod, mod, norm_g, w_in_p, w_s, bsb, ln_g, ln_b)


def _inproj_sample_kernel(x_ref, sh_ref, sc_ref, g_ref, w_ref, wd_ref, b0_ref, lng_ref, lnb_ref,
                          q_ref, kv0_ref, kv1_ref, kv2_ref, gm_ref, gvn_ref):
    h = _modulated_norm(x_ref, sh_ref, sc_ref, g_ref).astype(BF16)
    q_ref[...] = _dot(h, w_ref[:, Q_OFF:Q_OFF + ATT_WIDTH])
    for g, kv_ref in enumerate((kv0_ref, kv1_ref, kv2_ref)):
        off = KV_OFF + g * 2 * GROUP_W
        kv_ref[...] = _dot(h, w_ref[:, off:off + 2 * GROUP_W])
    gu = _gelu(_dot(h, w_ref[:, GU_OFF:GU_OFF + GM_WIDTH]))
    gv = _gelu(_dot(h, w_ref[:, GV_OFF:GV_OFF + GM_WIDTH]))
    for hh in range(GM_HEADS):
        sl = slice(hh * GM_HD, (hh + 1) * GM_HD)
        gvn = _gm_layernorm(gv, lng_ref, lnb_ref, hh)
        gvn_ref[:, sl] = gvn
        z = wd_ref[:, sl] * gvn + b0_ref[:, sl]
        gm_ref[:, sl] = (gu[:, sl] * z).astype(BF16)


def _inproj_sample(x, sh, sc, norm_g, w_in_p, wd, b0, ln_g, ln_b):
    n = x.shape[0]
    full = lambda shape: pl.BlockSpec(shape, lambda i: tuple(0 for _ in shape))
    return pl.pallas_call(
        _inproj_sample_kernel,
        grid=(1,),
        in_specs=[
            full((n, D_MODEL)), full((n, D_MODEL)), full((n, D_MODEL)),
            pl.BlockSpec((None, 1, D_MODEL), lambda i: (0, 0, 0)),
            full((D_MODEL, IN_WIDTH)),
            full((1, GM_WIDTH)), full((1, GM_WIDTH)),
            full((GM_HEADS, GM_HD)), full((GM_HEADS, GM_HD)),
        ],
        out_specs=[full((n, ATT_WIDTH)), full((n, 2 * GROUP_W)), full((n, 2 * GROUP_W)),
                   full((n, 2 * GROUP_W)), full((n, GM_WIDTH)), full((n, GM_WIDTH))],
        out_shape=[jax.ShapeDtypeStruct((n, ATT_WIDTH), F32)]
        + [jax.ShapeDtypeStruct((n, 2 * GROUP_W), F32)] * 3
        + [jax.ShapeDtypeStruct((n, GM_WIDTH), BF16), jax.ShapeDtypeStruct((n, GM_WIDTH), F32)],
        compiler_params=_cparams(("arbitrary",)),
        name="inproj_sample",
    )(x, sh, sc, norm_g, w_in_p, wd, b0, ln_g, ln_b)


def _attn_block(qt, kt, vt, off, nkeys):
    lane = lax.broadcasted_iota(jnp.int32, (1, LANES), 1)
    lo = lane < ATT_HD
    qi = lax.broadcasted_iota(jnp.int32, (QBLK, nkeys), 0)
    kk = lax.broadcasted_iota(jnp.int32, (QBLK, nkeys), 1)
    dist = qi - kk + off
    valid = lax.bitcast_convert_type(dist, jnp.uint32) <= jnp.uint32(BAND)
    res = []
    for sel in (lo, jnp.logical_not(lo)):
        qm = jnp.where(sel, qt, jnp.zeros_like(qt))
        s = jnp.where(valid, _dot_nt(qm, kt), NEG_INF)
        m = jnp.max(s, axis=-1, keepdims=True)
        p = jnp.exp(s - m)
        den = jnp.sum(p, axis=-1, keepdims=True)
        o = _dot(p.astype(BF16), vt) * (1.0 / den)
        res.append((o, m + jnp.log(den)))
    o = jnp.where(lo, res[0][0], res[1][0])
    lse = jnp.where(lo, res[0][1], res[1][1])
    return o, lse


def _attn_prompt_kernel(q0_ref, k0_ref, v0_ref, q1_ref, k1_ref, v1_ref, q2_ref, k2_ref, v2_ref,
                        a0_ref, a1_ref, a2_ref, of_scr, lf_scr, to_scr, tl_scr):
    seq = q0_ref.shape[0]

    def run_block(g, qt, kt, vt, off, nkeys, row0):
        o, lse = _attn_block(qt, kt, vt, off, nkeys)
        of_scr[g, pl.ds(row0, QBLK), :] = o
        lf_scr[g, pl.ds(row0, QBLK), :] = lse

    par = 4

    def g0_body(it, carry):
        for u in range(par):
            qb = it * par + u
            qs = pl.multiple_of(qb * QBLK, QBLK)
            ks = pl.multiple_of(jnp.maximum(qb - 1, 0) * QBLK, QBLK)
            run_block(0, q0_ref[pl.ds(qs, QBLK), :], k0_ref[pl.ds(ks, 2 * QBLK), :],
                      v0_ref[pl.ds(ks, 2 * QBLK), :], qs - ks, 2 * QBLK, qs)
        return carry

    lax.fori_loop(0, seq // QBLK // par, g0_body, 0)

    dil1 = DIL_PAIRS[1][1]
    len1 = seq // dil1

    def g1_body(r, carry):
        for qb in range(len1 // QBLK):
            qs = qb * QBLK
            ks = max(qb - 1, 0) * QBLK
            run_block(1, q1_ref[0, r, pl.ds(qs, QBLK), :], k1_ref[0, r, pl.ds(ks, 2 * QBLK), :],
                      v1_ref[0, r, pl.ds(ks, 2 * QBLK), :], qs - ks, 2 * QBLK,
                      pl.multiple_of(r * len1 + qs, QBLK))
        return carry

    lax.fori_loop(0, dil1, g1_body, 0)

    dil2 = DIL_PAIRS[2][1]
    len2 = seq // dil2

    def g2_body(it, carry):
        for u in range(par):
            r = it * par + u
            run_block(2, q2_ref[0, r], k2_ref[0, r], v2_ref[0, r], 0, QBLK,
                      pl.multiple_of(r * len2, QBLK))
        return carry

    lax.fori_loop(0, dil2 // par, g2_body, 0)

    def mix_body(nt, carry):
        t0 = pl.multiple_of(nt * QBLK, QBLK)
        o_nat = [of_scr[0, pl.ds(t0, QBLK), :]]
        l_nat = [lf_scr[0, pl.ds(t0, QBLK), :]]
        for g, dil, length in ((1, dil1, len1), (2, dil2, len2)):
            per = QBLK // dil
            for r in range(dil):
                src = pl.multiple_of(r * length + nt * per, per)
                to_scr[g - 1, pl.ds(r, per, stride=dil), :] = of_scr[g, pl.ds(src, per), :]
                tl_scr[g - 1, pl.ds(r, per, stride=dil), :] = lf_scr[g, pl.ds(src, per), :]
            o_nat.append(to_scr[g - 1])
            l_nat.append(tl_scr[g - 1])
        mx = jnp.maximum(jnp.maximum(l_nat[0], l_nat[1]), l_nat[2])
        e = [jnp.exp(l - mx) for l in l_nat]
        inv = 1.0 / (e[0] + e[1] + e[2])
        for g, a_ref in enumerate((a0_ref, a1_ref, a2_ref)):
            a_ref[pl.ds(t0, QBLK), :] = (o_nat[g] * (e[g] * inv)).astype(BF16)
        return carry

    lax.fori_loop(0, seq // QBLK, mix_body, 0)


def _attn_prompt(folded, batch, seq):
    n = batch * seq
    pairs = GROUP_W // LANES
    in_specs = []
    for _, dil in DIL_PAIRS:
        for _ in range(3):
            if dil == 1:
                in_specs.append(pl.BlockSpec((seq, LANES), lambda b, hp: (b, hp)))
            else:
                in_specs.append(pl.BlockSpec((1, dil, seq // dil, LANES), lambda b, hp: (b, 0, 0, hp)))
    return pl.pallas_call(
        _attn_prompt_kernel,
        grid=(batch, pairs),
        in_specs=in_specs,
        out_specs=[pl.BlockSpec((seq, LANES), lambda b, hp: (b, hp))] * N_DIL,
        out_shape=[jax.ShapeDtypeStruct((n, GROUP_W), BF16)] * N_DIL,
        scratch_shapes=[pltpu.VMEM((N_DIL, seq, LANES), F32), pltpu.VMEM((N_DIL, seq, LANES), F32),
                        pltpu.VMEM((N_DIL - 1, QBLK, LANES), F32), pltpu.VMEM((N_DIL - 1, QBLK, LANES), F32)],
        compiler_params=_cparams(("parallel", "parallel")),
        name="attn_prompt",
    )(*folded)


def _attn_sample_kernel(q_ref, n0_ref, n1_ref, n2_ref, c0_ref, c1_ref, c2_ref, o_ref):
    groups = ((n0_ref, c0_ref), (n1_ref, c1_ref), (n2_ref, c2_ref))

    def head(h, carry):
        outs, lses = [], []
        for g, (n_ref, c_ref) in enumerate(groups):
            dil = DIL_PAIRS[g][1]
            nbuf = c_ref.shape[-1]
            q = q_ref[0, g, h] * (ATT_HD ** -0.5)
            row = lax.broadcasted_iota(jnp.int32, (1, nbuf), 1)
            valid = jnp.bitwise_and(row, dil - 1) == 0
            s_buf = jnp.sum(c_ref[0, 0, h] * q, axis=0, keepdims=True)
            s_buf = jnp.where(valid, s_buf, NEG_INF)
            s_new = jnp.sum(n_ref[0, 0, h] * q, axis=0, keepdims=True)
            m = jnp.maximum(jnp.max(s_buf, axis=1, keepdims=True), s_new)
            p_buf = jnp.exp(s_buf - m)
            p_new = jnp.exp(s_new - m)
            den = jnp.sum(p_buf, axis=1, keepdims=True) + p_new
            o = jnp.sum(c_ref[0, 1, h] * p_buf, axis=1, keepdims=True) + n_ref[0, 1, h] * p_new
            outs.append(o / den)
            lses.append(m + jnp.log(den))
        mx = jnp.maximum(jnp.maximum(lses[0], lses[1]), lses[2])
        e = [jnp.exp(l - mx) for l in lses]
        inv = 1.0 / (e[0] + e[1] + e[2])
        for g in range(N_DIL):
            o_ref[0, g, h] = outs[g] * (e[g] * inv)
        return carry

    lax.fori_loop(0, ATT_HEADS, head, 0)


def _attn_sample(q, new_kvs, caches):
    n_real = q.shape[0]
    assert all(dil & (dil - 1) == 0 and c.shape[-1] == BAND * dil for c, (_, dil) in zip(caches, DIL_PAIRS))
    per_row = lambda a: pl.BlockSpec((1,) + a.shape[1:], lambda b: (b,) + (0,) * (a.ndim - 1))
    return pl.pallas_call(
        _attn_sample_kernel,
        grid=(n_real,),
        in_specs=[per_row(a) for a in (q, *new_kvs, *caches)],
        out_specs=per_row(q),
        out_shape=jax.ShapeDtypeStruct(q.shape, F32),
        compiler_params=_cparams(("arbitrary",)),
        name="attn_sample",
    )(q, *new_kvs, *caches)


def _topk_desc(sc, k):
    vals = []
    cur = sc
    for i in range(k):
        m = jnp.max(cur, axis=0, keepdims=True)
        vals.append(m)
        if i + 1 < k:
            cur = jnp.where(cur == m, NEG_INF, cur)
    return vals


def _outproj_peer_kernel(x_ref, a0_ref, a1_ref, a2_ref, gm_ref, gt1_ref, sh2_ref, sc2_ref, g1_ref, g2_ref,
                         wo_ref, wq_ref, keys_ref,
                         x1_ref, h2t_ref, s1_ref, s2_ref, e2_ref, thr_ref, lse_ref):
    mo = _dot(a0_ref[...], wo_ref[0:GROUP_W, :])
    mo += _dot(a1_ref[...], wo_ref[GROUP_W:2 * GROUP_W, :])
    mo += _dot(a2_ref[...], wo_ref[2 * GROUP_W:3 * GROUP_W, :])
    mo += _dot(gm_ref[...], wo_ref[3 * GROUP_W:4 * GROUP_W, :])
    x1 = x_ref[...] + gt1_ref[...] * _rms(mo, g1_ref[...])
    x1_ref[...] = x1
    h2 = _rms(x1, g2_ref[...]) * (1.0 + sc2_ref[...]) + sh2_ref[...]
    h2t_ref[...] = jnp.transpose(h2).astype(BF16)
    qr = _dot(h2.astype(BF16), wq_ref[...])
    lane_groups = thr_ref.shape[0]
    for h in range(PEER_HEADS):
        tops = []
        for p, s_ref in enumerate((s1_ref, s2_ref)):
            c0 = (h * 2 + p) * PEER_HALF
            sc = _dot_nt(keys_ref[p], qr[:, c0:c0 + PEER_HALF].astype(BF16)) * LOG2E
            for lg in range(lane_groups):
                s_ref[lg, h] = sc[:, lg * LANES:(lg + 1) * LANES]
            tops.append(_topk_desc(sc, PEER_TOPK))
            if p == 1:
                e2 = jnp.exp2(sc - tops[1][0])
                for lg in range(lane_groups):
                    e2_ref[lg, h] = e2[:, lg * LANES:(lg + 1) * LANES]
        t1 = jnp.concatenate(tops[0], axis=0)
        half = PEER_TOPK // 2
        cand = jnp.concatenate(
            [t1 + tops[1][0]]
            + [t1[:half] + tops[1][b] for b in range(1, half)]
            + [jnp.concatenate(tops[1][half:], axis=0) + tops[0][0]], axis=0)
        best = _topk_desc(cand, PEER_TOPK)
        z = jnp.ones_like(best[0])
        for v in best[1:]:
            z = z + jnp.exp2(v - best[0])
        lse = best[0] + jnp.log(z) * LOG2E - tops[1][0]
        for lg in range(lane_groups):
            thr_ref[lg, pl.ds(h, 1), :] = best[-1][:, lg * LANES:(lg + 1) * LANES]
            lse_ref[lg, pl.ds(h, 1), :] = lse[:, lg * LANES:(lg + 1) * LANES]


def _outproj_peer(x, atts, gm, mods, norm_g, w_out, w_q, keys, tm, per_row_mod, tiles_per_seq):
    n = x.shape[0]
    if per_row_mod:
        mod_specs = [pl.BlockSpec((tm, D_MODEL), lambda i: (i, 0))] * 3
    else:
        mod_specs = [pl.BlockSpec((None, 1, D_MODEL), lambda i, k=k: ((i // tiles_per_seq) * 6 + k, 0, 0))
                     for k in (2, 3, 4)]
    row_spec = lambda w: pl.BlockSpec((tm, w), lambda i: (i, 0))
    const = lambda shape: pl.BlockSpec(shape, lambda i: tuple(0 for _ in shape))
    return pl.pallas_call(
        _outproj_peer_kernel,
        grid=(n // tm,),
        in_specs=[row_spec(D_MODEL)] + [row_spec(GROUP_W)] * 4 + mod_specs
        + [pl.BlockSpec((None, 1, D_MODEL), lambda i: (1, 0, 0)),
           pl.BlockSpec((None, 1, D_MODEL), lambda i: (2, 0, 0)),
           const((D_MODEL, D_MODEL)), const((D_MODEL, D_MODEL)), const((2, N_KEYS, PEER_HALF))],
        out_specs=[row_spec(D_MODEL),
                   pl.BlockSpec((D_MODEL, tm), lambda i: (0, i)),
                   pl.BlockSpec((tm // LANES, PEER_HEADS, N_KEYS, LANES), lambda i: (i, 0, 0, 0)),
                   pl.BlockSpec((tm // LANES, PEER_HEADS, N_KEYS, LANES), lambda i: (i, 0, 0, 0)),
                   pl.BlockSpec((tm // LANES, PEER_HEADS, N_KEYS, LANES), lambda i: (i, 0, 0, 0)),
                   pl.BlockSpec((tm // LANES, PEER_HEADS, LANES), lambda i: (i, 0, 0)),
                   pl.BlockSpec((tm // LANES, PEER_HEADS, LANES), lambda i: (i, 0, 0))],
        out_shape=[jax.ShapeDtypeStruct((n, D_MODEL), F32),
                   jax.ShapeDtypeStruct((D_MODEL, n), BF16),
                   jax.ShapeDtypeStruct((n // LANES, PEER_HEADS, N_KEYS, LANES), F32),
                   jax.ShapeDtypeStruct((n // LANES, PEER_HEADS, N_KEYS, LANES), F32),
                   jax.ShapeDtypeStruct((n // LANES, PEER_HEADS, N_KEYS, LANES), F32),
                   jax.ShapeDtypeStruct((n // LANES, PEER_HEADS, LANES), F32),
                   jax.ShapeDtypeStruct((n // LANES, PEER_HEADS, LANES), F32)],
        compiler_params=_cparams(("parallel",)),
        name="outproj_peer",
    )(x, *atts, gm, *mods, norm_g, norm_g, w_out, w_q, keys)


def _expert_kernel(h2t_ref, u_ref, vt_ref, s1_ref, s2_ref, e2_ref, thr_ref, lse_ref, o_ref,
                   acc_scr, g_scr, w_scr, rowb_scr, thrb_scr, tile_scr):
    j = pl.program_id(1)
    n_tiles = pl.num_programs(1) - 1
    te = u_ref.shape[0]
    rows_per_tile = te // N_KEYS
    lane_groups = s1_ref.shape[0]
    n_rows = rows_per_tile * PEER_HEADS
    ch = 8

    @pl.when(j == 0)
    def _():
        acc_scr[...] = jnp.zeros_like(acc_scr)
        g_scr[...] = jnp.zeros_like(g_scr)
        for lg in range(lane_groups):
            for h in range(PEER_HEADS):
                thrb_scr[lg, h] = jnp.broadcast_to(thr_ref[lg, pl.ds(h, 1), :], (ch, LANES))

    jt = jnp.minimum(j, n_tiles - 1)
    for lg in range(lane_groups):
        for rr in range(rows_per_tile):
            tile = tile_scr.at[lg * rows_per_tile + rr]
            for h in range(PEER_HEADS):
                tile[pl.ds(h, 1), :] = s1_ref[lg, h, pl.ds(jt * rows_per_tile + rr, 1), :]
            rows8 = tile[...]
            e8 = jnp.exp2(rows8 - lse_ref[lg])
            for h in range(PEER_HEADS):
                rowb_scr[lg, rr * PEER_HEADS + h] = jnp.broadcast_to(rows8[h:h + 1], (ch, LANES))
                rowb_scr[lg, n_rows + rr * PEER_HEADS + h] = jnp.broadcast_to(e8[h:h + 1], (ch, LANES))

    def build_gates(lg):
        for cc in range(N_KEYS // ch):
            gates = [None] * rows_per_tile
            for h in range(PEER_HEADS):
                b = s2_ref[lg, h, cc * ch:(cc + 1) * ch, :]
                eb = e2_ref[lg, h, cc * ch:(cc + 1) * ch, :]
                t = thrb_scr[lg, h]
                for rr in range(rows_per_tile):
                    a = rowb_scr[lg, rr * PEER_HEADS + h]
                    ea = rowb_scr[lg, n_rows + rr * PEER_HEADS + h]
                    term = jnp.where(a + b >= t, ea * eb, 0.0)
                    gates[rr] = term if gates[rr] is None else gates[rr] + term
            for rr in range(rows_per_tile):
                g_scr[lg, rr * N_KEYS + cc * ch:rr * N_KEYS + (cc + 1) * ch, :] = gates[rr]

    act = _dot(u_ref[...], h2t_ref[...])
    for lg in range(lane_groups):
        lanes = slice(lg * LANES, (lg + 1) * LANES)
        w_scr[:, lanes] = (g_scr[lg] * _gelu_sigmoid(act[:, lanes])).astype(BF16)

    d_slab = acc_scr.shape[0] // lane_groups

    def piece(lg, carry):
        d0 = pl.multiple_of(lg * d_slab, d_slab)
        acc_scr[pl.ds(d0, d_slab), :] += _dot(vt_ref[pl.ds(d0, d_slab), :], w_scr[...])
        build_gates(lg)
        return carry

    lax.fori_loop(0, lane_groups, piece, 0)


    @pl.when(j == n_tiles)
    def _():
        o_ref[...] = acc_scr[...]


def _experts(h2t, u_bf, vt_bf, s1, s2, e2, thr, lse, tm):
    n = h2t.shape[1]
    te = 512
    n_tiles = N_EXPERTS // te
    prev_tile = lambda j: jnp.maximum(j - 1, 0)
    return pl.pallas_call(
        _expert_kernel,
        grid=(n // tm, n_tiles + 1),
        in_specs=[
            pl.BlockSpec((D_MODEL, tm), lambda i, j: (0, i)),
            pl.BlockSpec((te, D_MODEL), lambda i, j: (prev_tile(j), 0)),
            pl.BlockSpec((D_MODEL, te), lambda i, j: (0, prev_tile(j))),
            pl.BlockSpec((tm // LANES, PEER_HEADS, N_KEYS, LANES), lambda i, j: (i, 0, 0, 0)),
            pl.BlockSpec((tm // LANES, PEER_HEADS, N_KEYS, LANES), lambda i, j: (i, 0, 0, 0)),
            pl.BlockSpec((tm // LANES, PEER_HEADS, N_KEYS, LANES), lambda i, j: (i, 0, 0, 0)),
            pl.BlockSpec((tm // LANES, PEER_HEADS, LANES), lambda i, j: (i, 0, 0)),
            pl.BlockSpec((tm // LANES, PEER_HEADS, LANES), lambda i, j: (i, 0, 0)),
        ],
        out_specs=pl.BlockSpec((D_MODEL, tm), lambda i, j: (0, i)),
        out_shape=jax.ShapeDtypeStruct((D_MODEL, n), F32),
        scratch_shapes=[pltpu.VMEM((D_MODEL, tm), F32), pltpu.VMEM((tm // LANES, te, LANES), F32),
                        pltpu.VMEM((te, tm), BF16),
                        pltpu.VMEM((tm // LANES, 2 * (te // N_KEYS) * PEER_HEADS, 8, LANES), F32),
                        pltpu.VMEM((tm // LANES, PEER_HEADS, 8, LANES), F32),
                        pltpu.VMEM((tm // LANES * (te // N_KEYS), PEER_HEADS, LANES), F32)],
        compiler_params=_cparams(("parallel", "arbitrary")),
        name="peer_experts",
    )(h2t, u_bf, vt_bf, s1, s2, e2, thr, lse)


def _final_kernel(x1_ref, pt_ref, gt2_ref, g3_ref, y_ref):
    peer = jnp.transpose(pt_ref[...])
    y_ref[...] = x1_ref[...] + gt2_ref[...] * _rms(peer, g3_ref[...])


def _final(x1, peer_t, gt2, norm_g, tm, per_row_mod, tiles_per_seq):
    n = x1.shape[0]
    if per_row_mod:
        gt2_spec = pl.BlockSpec((tm, D_MODEL), lambda i: (i, 0))
    else:
        gt2_spec = pl.BlockSpec((None, 1, D_MODEL), lambda i: ((i // tiles_per_seq) * 6 + 5, 0, 0))
    return pl.pallas_call(
        _final_kernel,
        grid=(n // tm,),
        in_specs=[pl.BlockSpec((tm, D_MODEL), lambda i: (i, 0)),
                  pl.BlockSpec((D_MODEL, tm), lambda i: (0, i)),
                  gt2_spec,
                  pl.BlockSpec((None, 1, D_MODEL), lambda i: (3, 0, 0))],
        out_specs=pl.BlockSpec((tm, D_MODEL), lambda i: (i, 0)),
        out_shape=jax.ShapeDtypeStruct((n, D_MODEL), F32),
        compiler_params=_cparams(("parallel",)),
        name="final_residual",
    )(x1, peer_t, gt2, norm_g)


def _permute_w_in(w_in):
    q = w_in[:, :ATT_WIDTH]
    k = w_in[:, ATT_WIDTH:2 * ATT_WIDTH]
    v = w_in[:, 2 * ATT_WIDTH:3 * ATT_WIDTH]
    parts = [q]
    for g in range(N_DIL):
        parts += [k[:, g * GROUP_W:(g + 1) * GROUP_W], v[:, g * GROUP_W:(g + 1) * GROUP_W]]
    parts.append(w_in[:, 3 * ATT_WIDTH:])
    return jnp.concatenate(parts, axis=1).astype(BF16)


def kernel(x_prompt, x_sample, c_prompt, c_sample, cache_kv_dil1, cache_kv_dil4, cache_kv_dil16, w_ada, b_ada, norm_g, w_in, gm_ln_g, gm_ln_b, w_s, b_s, w_out, w_q, sub_keys, expert_u, expert_v):
    batch, seq, _ = x_prompt.shape
    n_s = x_sample.shape[0] * x_sample.shape[1]
    assert w_ada.shape[0] == 1 and x_sample.shape[1] == 1
    n_p = batch * seq
    pad_s = 128

    w_in_p = _permute_w_in(w_in[0])
    w_out_b = w_out[0].astype(BF16)
    w_q_b = w_q[0].astype(BF16)
    keys_b = sub_keys[0].astype(BF16)
    u_b = expert_u[0].astype(BF16)
    vt_b = jnp.transpose(expert_v[0].astype(BF16))
    ng = norm_g[0].reshape(4, 1, D_MODEL)
    ln_g, ln_b = gm_ln_g[0], gm_ln_b[0]

    rows = batch + n_s
    c_all = jnp.zeros((16, D_MODEL), F32).at[:rows].set(jnp.concatenate([c_prompt, c_sample], axis=0))
    mod = _ada(c_all, w_ada[0], b_ada)
    mod_p = mod.reshape(16 * 6, 1, D_MODEL)
    mod_s = jnp.zeros((pad_s, 6, D_MODEL), F32).at[:n_s].set(mod[batch:rows].reshape(n_s, 6, D_MODEL))
    mods_s = [mod_s[:, k] for k in range(6)]

    xp = x_prompt.reshape(n_p, D_MODEL)
    bsb = jnp.broadcast_to(b_s[0][:, :, None], (GM_HEADS, CHUNK, GM_HD))
    outs = _inproj_prompt(xp, mod_p, ng, w_in_p, w_s[0], bsb, ln_g, ln_b, batch, seq)
    folded, kv_p, gm_p = outs[:9], outs[9:12], outs[12]
    att_p = _attn_prompt(folded, batch, seq)
    tm_p = 256
    x1_p, h2t_p, *sel_p = _outproj_peer(
        xp, att_p, gm_p, [mod_p] * 3, ng, w_out_b, w_q_b, keys_b, tm_p, False, seq // tm_p)
    peer_t_p = _experts(h2t_p, u_b, vt_b, *sel_p, 512)
    y_p = _final(x1_p, peer_t_p, mod_p, ng, tm_p, False, seq // tm_p)

    xs = jnp.zeros((pad_s, D_MODEL), F32).at[:n_s].set(x_sample.reshape(n_s, D_MODEL))
    wd = jnp.repeat(w_s[0][:, 0, 0], GM_HD)[None, :]
    b0 = jnp.repeat(b_s[0][:, 0], GM_HD)[None, :]
    q_s, kv0_s, kv1_s, kv2_s, gm_s, gvn_s = _inproj_sample(
        xs, mods_s[0], mods_s[1], ng, w_in_p, wd, b0, ln_g, ln_b)
    new_kv_s = [kv[:n_s].reshape(n_s, 2, ATT_HEADS, ATT_HD) for kv in (kv0_s, kv1_s, kv2_s)]
    caches_t = [jnp.transpose(c[0], (0, 2, 3, 4, 1)) for c in (cache_kv_dil1, cache_kv_dil4, cache_kv_dil16)]
    att_s = _attn_sample(q_s[:n_s].reshape(n_s, N_DIL, ATT_HEADS, ATT_HD, 1),
                         [kv[..., None] for kv in new_kv_s], caches_t)
    att_s = att_s.reshape(n_s, N_DIL, GROUP_W).astype(BF16)
    att_s = [jnp.zeros((pad_s, GROUP_W), BF16).at[:n_s].set(att_s[:, g]) for g in range(N_DIL)]
    x1_s, h2t_s, *sel_s = _outproj_peer(
        xs, att_s, gm_s, mods_s[2:5], ng, w_out_b, w_q_b, keys_b, pad_s, True, 1)
    peer_t_s = _experts(h2t_s, u_b, vt_b, *sel_s, pad_s)
    y_s = _final(x1_s, peer_t_s, mods_s[5], ng, pad_s, True, 1)

    y_prompt = y_p.reshape(batch, seq, D_MODEL)
    y_sample = y_s[:n_s].reshape(x_sample.shape)
    new_kv_p = []
    for kv_t, (win, _) in zip(kv_p, DIL_PAIRS):
        keep = min(win, seq)
        kv_t = kv_t[:, :, seq - keep:].reshape(batch, 2, ATT_HEADS, ATT_HD, keep)
        new_kv_p.append(jnp.transpose(kv_t, (0, 4, 1, 2, 3))[None])
    new_kv_s = [kv.reshape(1, n_s, 1, 2, ATT_HEADS, ATT_HD) for kv in new_kv_s]
    state_gv = gvn_s[:n_s].reshape(1, n_s, 1, GM_HEADS, GM_HD)
    return (y_prompt, y_sample, *new_kv_p, *new_kv_s, state_gv)
```

```python
import functools
import math

import jax
import jax.numpy as jnp
from jax import lax
from jax.experimental import pallas as pl
from jax.experimental.pallas import tpu as pltpu

F32 = jnp.float32
BF16 = jnp.bfloat16

D_MODEL = 2048
ATT_HD = 64
ATT_HEADS = 8
GROUP_W = ATT_HEADS * ATT_HD
DIL_PAIRS = ((128, 1), (512, 4), (2048, 16))
N_DIL = 3
ATT_WIDTH = N_DIL * GROUP_W
QBLK = 128
BAND = 128
CHUNK = 128
GM_HEADS = 4
GM_HD = 128
GM_WIDTH = GM_HEADS * GM_HD
IN_WIDTH = 3 * ATT_WIDTH + 2 * GM_WIDTH
N_KEYS = 128
N_EXPERTS = N_KEYS * N_KEYS
PEER_HEADS = 8
PEER_HALF = 128
PEER_TOPK = 16
EPS = 1e-6
NEG_INF = float("-inf")
LOG2E = 1.4426950408889634

LANES = 128
VMEM_LIMIT = 56 * 1024 * 1024

Q_OFF = 0
KV_OFF = ATT_WIDTH
GU_OFF = ATT_WIDTH + N_DIL * 2 * GROUP_W
GV_OFF = GU_OFF + GM_WIDTH


def _cparams(sem, flags=None):
    return pltpu.CompilerParams(dimension_semantics=sem, vmem_limit_bytes=VMEM_LIMIT, flags=flags)


def _gelu(x):
    c = math.sqrt(2.0 / math.pi)
    return 0.5 * x * (1.0 + jnp.tanh(c * (x + 0.044715 * (x * x * x))))


def _gelu_sigmoid(x):
    k2 = -2.0 * math.sqrt(2.0 / math.pi) * LOG2E
    k1 = k2 * 0.044715
    return x / (1.0 + jnp.exp2(x * (k2 + k1 * (x * x))))


def _rms(x, g):
    return x * lax.rsqrt(jnp.mean(x * x, axis=-1, keepdims=True) + EPS) * g


def _dot(a, b):
    return jnp.dot(a, b, preferred_element_type=F32)


def _dot_nt(a, b):
    return lax.dot_general(a, b, (((1,), (1,)), ((), ())), preferred_element_type=F32)


def _ada_kernel(c_ref, w_ref, b_ref, o_ref):
    c = c_ref[...]
    act = (c / (1.0 + jnp.exp(-c))).astype(BF16)
    o_ref[...] = _dot(act, w_ref[...].astype(BF16)) + b_ref[...]


def _ada(c_all, w_ada, b_ada):
    rows = c_all.shape[0]
    n_out = w_ada.shape[1]
    tn = 1024
    return pl.pallas_call(
        _ada_kernel,
        grid=(n_out // tn,),
        in_specs=[
            pl.BlockSpec((rows, D_MODEL), lambda j: (0, 0)),
            pl.BlockSpec((D_MODEL, tn), lambda j: (0, j)),
            pl.BlockSpec((1, tn), lambda j: (0, j)),
        ],
        out_specs=pl.BlockSpec((rows, tn), lambda j: (0, j)),
        out_shape=jax.ShapeDtypeStruct((rows, n_out), F32),
        compiler_params=_cparams(("parallel",)),
        name="ada_mod",
    )(c_all, w_ada, b_ada)


def _modulated_norm(x_ref, sh_ref, sc_ref, g_ref):
    x = x_ref[...]
    return _rms(x, g_ref[...]) * (1.0 + sc_ref[...]) + sh_ref[...]


def _gm_layernorm(gv, lng_ref, lnb_ref, hh):
    v = gv[:, hh * GM_HD:(hh + 1) * GM_HD]
    mu = jnp.mean(v, axis=-1, keepdims=True)
    var = jnp.mean(jnp.square(v - mu), axis=-1, keepdims=True)
    return (v - mu) * lax.rsqrt(var + EPS) * lng_ref[pl.ds(hh, 1), :] + lnb_ref[pl.ds(hh, 1), :]


def _inproj_prompt_kernel(x_ref, sh_ref, sc_ref, g_ref, w_ref, ws_ref, bsb_ref, lng_ref, lnb_ref,
                          q0_ref, k0_ref, v0_ref, q1_ref, k1_ref, v1_ref, q2_ref, k2_ref, v2_ref,
                          kv0_ref, kv1_ref, kv2_ref, gm_ref, h_scr, y_scr):
    tm = x_ref.shape[0]
    h_scr[...] = _modulated_norm(x_ref, sh_ref, sc_ref, g_ref).astype(BF16)

    def proj(off):
        return _dot(h_scr[...], w_ref[:, off:off + GROUP_W])

    folded = ((q0_ref, k0_ref, v0_ref), (q1_ref, k1_ref, v1_ref), (q2_ref, k2_ref, v2_ref))
    kv_refs = (kv0_ref, kv1_ref, kv2_ref)
    for g, (_, dil) in enumerate(DIL_PAIRS):
        offs = (Q_OFF + g * GROUP_W, KV_OFF + g * 2 * GROUP_W, KV_OFF + g * 2 * GROUP_W + GROUP_W)
        for which, off in enumerate(offs):
            y = proj(off)
            if which > 0:
                kv_refs[g][0, (which - 1) * GROUP_W:which * GROUP_W, :] = jnp.transpose(y)
            else:
                y = y * (ATT_HD ** -0.5)
            dst = folded[g][which]
            if dil == 1:
                dst[...] = y.astype(BF16)
            else:
                for cb in range(GROUP_W // LANES):
                    y_scr[cb] = y[:, cb * LANES:(cb + 1) * LANES]
                for r in range(dil):
                    for cb in range(GROUP_W // LANES):
                        dst[0, r, :, cb * LANES:(cb + 1) * LANES] = (
                            y_scr[cb, pl.ds(r, tm // dil, stride=dil), :].astype(BF16))

    gu = _gelu(proj(GU_OFF))
    gv = _gelu(proj(GV_OFF))
    row = lax.broadcasted_iota(jnp.int32, (CHUNK, CHUNK), 0)
    col = lax.broadcasted_iota(jnp.int32, (CHUNK, CHUNK), 1)
    for hh in range(GM_HEADS):
        gvn = _gm_layernorm(gv, lng_ref, lnb_ref, hh).astype(BF16)
        w_tril = jnp.where(row >= col, ws_ref[hh], 0.0).astype(BF16)
        for ci in range(tm // CHUNK):
            z = _dot(w_tril, gvn[ci * CHUNK:(ci + 1) * CHUNK]) + bsb_ref[hh]
            gu_blk = gu[ci * CHUNK:(ci + 1) * CHUNK, hh * GM_HD:(hh + 1) * GM_HD]
            gm_ref[ci * CHUNK:(ci + 1) * CHUNK, hh * GM_HD:(hh + 1) * GM_HD] = (gu_blk * z).astype(BF16)


def _inproj_prompt(x, mod, norm_g, w_in_p, w_s, bsb, ln_g, ln_b, batch, seq):
    n = x.shape[0]
    tm = 256
    tiles_per_seq = seq // tm

    def mod_spec(k):
        return pl.BlockSpec((None, 1, D_MODEL), lambda i: ((i // tiles_per_seq) * 6 + k, 0, 0))

    def folded_spec(dil):
        if dil == 1:
            return pl.BlockSpec((tm, GROUP_W), lambda i: (i, 0))
        return pl.BlockSpec((1, dil, tm // dil, GROUP_W),
                            lambda i: (i // tiles_per_seq, 0, i % tiles_per_seq, 0))

    def folded_shape(dil):
        if dil == 1:
            return jax.ShapeDtypeStruct((n, GROUP_W), BF16)
        return jax.ShapeDtypeStruct((batch, dil, seq // dil, GROUP_W), BF16)

    out_specs, out_shape = [], []
    for _, dil in DIL_PAIRS:
        for _ in range(3):
            out_specs.append(folded_spec(dil))
            out_shape.append(folded_shape(dil))
    for _ in range(N_DIL):
        out_specs.append(pl.BlockSpec((1, 2 * GROUP_W, tm),
                                      lambda i: (i // tiles_per_seq, 0, i % tiles_per_seq)))
        out_shape.append(jax.ShapeDtypeStruct((batch, 2 * GROUP_W, seq), F32))
    out_specs.append(pl.BlockSpec((tm, GM_WIDTH), lambda i: (i, 0)))
    out_shape.append(jax.ShapeDtypeStruct((n, GM_WIDTH), BF16))

    return pl.pallas_call(
        _inproj_prompt_kernel,
        grid=(n // tm,),
        in_specs=[
            pl.BlockSpec((tm, D_MODEL), lambda i: (i, 0)),
            mod_spec(0), mod_spec(1),
            pl.BlockSpec((None, 1, D_MODEL), lambda i: (0, 0, 0)),
            pl.BlockSpec((D_MODEL, IN_WIDTH), lambda i: (0, 0)),
            pl.BlockSpec((GM_HEADS, CHUNK, CHUNK), lambda i: (0, 0, 0)),
            pl.BlockSpec((GM_HEADS, CHUNK, GM_HD), lambda i: (0, 0, 0)),
            pl.BlockSpec((GM_HEADS, GM_HD), lambda i: (0, 0)),
            pl.BlockSpec((GM_HEADS, GM_HD), lambda i: (0, 0)),
        ],
        out_specs=out_specs,
        out_shape=out_shape,
        scratch_shapes=[pltpu.VMEM((tm, D_MODEL), BF16), pltpu.VMEM((GROUP_W // LANES, tm, LANES), F32)],
        compiler_params=_cparams(("parallel",)),
        name="inproj_prompt",
    )(x, mod, mod, norm_g, w_in_p, w_s, bsb, ln_g, ln_b)


def _inproj_sample_kernel(x_ref, sh_ref, sc_ref, g_ref, w_ref, wd_ref, b0_ref, lng_ref, lnb_ref,
                          q_ref, kv0_ref, kv1_ref, kv2_ref, gm_ref, gvn_ref):
    h = _modulated_norm(x_ref, sh_ref, sc_ref, g_ref).astype(BF16)
    q_ref[...] = _dot(h, w_ref[:, Q_OFF:Q_OFF + ATT_WIDTH])
    for g, kv_ref in enumerate((kv0_ref, kv1_ref, kv2_ref)):
        off = KV_OFF + g * 2 * GROUP_W
        kv_ref[...] = _dot(h, w_ref[:, off:off + 2 * GROUP_W])
    gu = _gelu(_dot(h, w_ref[:, GU_OFF:GU_OFF + GM_WIDTH]))
    gv = _gelu(_dot(h, w_ref[:, GV_OFF:GV_OFF + GM_WIDTH]))
    for hh in range(GM_HEADS):
        sl = slice(hh * GM_HD, (hh + 1) * GM_HD)
        gvn = _gm_layernorm(gv, lng_ref, lnb_ref, hh)
        gvn_ref[:, sl] = gvn
        z = wd_ref[:, sl] * gvn + b0_ref[:, sl]
        gm_ref[:, sl] = (gu[:, sl] * z).astype(BF16)


def _inproj_sample(x, sh, sc, norm_g, w_in_p, wd, b0, ln_g, ln_b):
    n = x.shape[0]
    full = lambda shape: pl.BlockSpec(shape, lambda i: tuple(0 for _ in shape))
    return pl.pallas_call(
        _inproj_sample_kernel,
        grid=(1,),
        in_specs=[
            full((n, D_MODEL)), full((n, D_MODEL)), full((n, D_MODEL)),
            pl.BlockSpec((None, 1, D_MODEL), lambda i: (0, 0, 0)),
            full((D_MODEL, IN_WIDTH)),
            full((1, GM_WIDTH)), full((1, GM_WIDTH)),
            full((GM_HEADS, GM_HD)), full((GM_HEADS, GM_HD)),
        ],
        out_specs=[full((n, ATT_WIDTH)), full((n, 2 * GROUP_W)), full((n, 2 * GROUP_W)),
                   full((n, 2 * GROUP_W)), full((n, GM_WIDTH)), full((n, GM_WIDTH))],
        out_shape=[jax.ShapeDtypeStruct((n, ATT_WIDTH), F32)]
        + [jax.ShapeDtypeStruct((n, 2 * GROUP_W), F32)] * 3
        + [jax.ShapeDtypeStruct((n, GM_WIDTH), BF16), jax.ShapeDtypeStruct((n, GM_WIDTH), F32)],
        compiler_params=_cparams(("arbitrary",)),
        name="inproj_sample",
    )(x, sh, sc, norm_g, w_in_p, wd, b0, ln_g, ln_b)


def _attn_block(qt, kt, vt, off, nkeys):
    lane = lax.broadcasted_iota(jnp.int32, (1, LANES), 1)
    lo = lane < ATT_HD
    qi = lax.broadcasted_iota(jnp.int32, (QBLK, nkeys), 0)
    kk = lax.broadcasted_iota(jnp.int32, (QBLK, nkeys), 1)
    dist = qi - kk + off
    valid = lax.bitcast_convert_type(dist, jnp.uint32) <= jnp.uint32(BAND)
    res = []
    for sel in (lo, jnp.logical_not(lo)):
        qm = jnp.where(sel, qt, jnp.zeros_like(qt))
        s = jnp.where(valid, _dot_nt(qm, kt), NEG_INF)
        m = jnp.max(s, axis=-1, keepdims=True)
        p = jnp.exp(s - m)
        den = jnp.sum(p, axis=-1, keepdims=True)
        o = _dot(p.astype(BF16), vt) * (1.0 / den)
        res.append((o, m + jnp.log(den)))
    o = jnp.where(lo, res[0][0], res[1][0])
    lse = jnp.where(lo, res[0][1], res[1][1])
    return o, lse


def _attn_prompt_kernel(q0_ref, k0_ref, v0_ref, q1_ref, k1_ref, v1_ref, q2_ref, k2_ref, v2_ref,
                        a0_ref, a1_ref, a2_ref, of_scr, lf_scr, to_scr, tl_scr):
    seq = q0_ref.shape[0]

    def run_block(g, qt, kt, vt, off, nkeys, row0):
        o, lse = _attn_block(qt, kt, vt, off, nkeys)
        of_scr[g, pl.ds(row0, QBLK), :] = o
        lf_scr[g, pl.ds(row0, QBLK), :] = lse

    par = 4

    def g0_body(it, carry):
        for u in range(par):
            qb = it * par + u
            qs = pl.multiple_of(qb * QBLK, QBLK)
            ks = pl.multiple_of(jnp.maximum(qb - 1, 0) * QBLK, QBLK)
            run_block(0, q0_ref[pl.ds(qs, QBLK), :], k0_ref[pl.ds(ks, 2 * QBLK), :],
                      v0_ref[pl.ds(ks, 2 * QBLK), :], qs - ks, 2 * QBLK, qs)
        return carry

    lax.fori_loop(0, seq // QBLK // par, g0_body, 0)

    dil1 = DIL_PAIRS[1][1]
    len1 = seq // dil1

    def g1_body(r, carry):
        for qb in range(len1 // QBLK):
            qs = qb * QBLK
            ks = max(qb - 1, 0) * QBLK
            run_block(1, q1_ref[0, r, pl.ds(qs, QBLK), :], k1_ref[0, r, pl.ds(ks, 2 * QBLK), :],
                      v1_ref[0, r, pl.ds(ks, 2 * QBLK), :], qs - ks, 2 * QBLK,
                      pl.multiple_of(r * len1 + qs, QBLK))
        return carry

    lax.fori_loop(0, dil1, g1_body, 0)

    dil2 = DIL_PAIRS[2][1]
    len2 = seq // dil2

    def g2_body(it, carry):
        for u in range(par):
            r = it * par + u
            pad = it * par + (u ^ 1)
            kt = jnp.concatenate([k2_ref[0, r], k2_ref[0, pad]], axis=0)
            vt = jnp.concatenate([v2_ref[0, r], v2_ref[0, pad]], axis=0)
            run_block(2, q2_ref[0, r], kt, vt, 0, 2 * QBLK, pl.multiple_of(r * len2, QBLK))
        return carry

    lax.fori_loop(0, dil2 // par, g2_body, 0)

    def mix_body(nt, carry):
        t0 = pl.multiple_of(nt * QBLK, QBLK)
        o_nat = [of_scr[0, pl.ds(t0, QBLK), :]]
        l_nat = [lf_scr[0, pl.ds(t0, QBLK), :]]
        for g, dil, length in ((1, dil1, len1), (2, dil2, len2)):
            per = QBLK // dil
            for r in range(dil):
                src = pl.multiple_of(r * length + nt * per, per)
                to_scr[g - 1, pl.ds(r, per, stride=dil), :] = of_scr[g, pl.ds(src, per), :]
                tl_scr[g - 1, pl.ds(r, per, stride=dil), :] = lf_scr[g, pl.ds(src, per), :]
            o_nat.append(to_scr[g - 1])
            l_nat.append(tl_scr[g - 1])
        mx = jnp.maximum(jnp.maximum(l_nat[0], l_nat[1]), l_nat[2])
        e = [jnp.exp(l - mx) for l in l_nat]
        inv = 1.0 / (e[0] + e[1] + e[2])
        for g, a_ref in enumerate((a0_ref, a1_ref, a2_ref)):
            a_ref[pl.ds(t0, QBLK), :] = (o_nat[g] * (e[g] * inv)).astype(BF16)
        return carry

    lax.fori_loop(0, seq // QBLK, mix_body, 0)


def _attn_prompt(folded, batch, seq):
    n = batch * seq
    pairs = GROUP_W // LANES
    in_specs = []
    for _, dil in DIL_PAIRS:
        for _ in range(3):
            if dil == 1:
                in_specs.append(pl.BlockSpec((seq, LANES), lambda b, hp: (b, hp)))
            else:
                in_specs.append(pl.BlockSpec((1, dil, seq // dil, LANES), lambda b, hp: (b, 0, 0, hp)))
    return pl.pallas_call(
        _attn_prompt_kernel,
        grid=(batch, pairs),
        in_specs=in_specs,
        out_specs=[pl.BlockSpec((seq, LANES), lambda b, hp: (b, hp))] * N_DIL,
        out_shape=[jax.ShapeDtypeStruct((n, GROUP_W), BF16)] * N_DIL,
        scratch_shapes=[pltpu.VMEM((N_DIL, seq, LANES), F32), pltpu.VMEM((N_DIL, seq, LANES), F32),
                        pltpu.VMEM((N_DIL - 1, QBLK, LANES), F32), pltpu.VMEM((N_DIL - 1, QBLK, LANES), F32)],
        compiler_params=_cparams(("parallel", "parallel")),
        name="attn_prompt",
    )(*folded)


def _attn_sample_kernel(q_ref, n0_ref, n1_ref, n2_ref, c0_ref, c1_ref, c2_ref, o_ref):
    groups = ((n0_ref, c0_ref), (n1_ref, c1_ref), (n2_ref, c2_ref))

    def head(h, carry):
        outs, lses = [], []
        for g, (n_ref, c_ref) in enumerate(groups):
            dil = DIL_PAIRS[g][1]
            nbuf = c_ref.shape[-1]
            q = q_ref[0, g, h] * (ATT_HD ** -0.5)
            row = lax.broadcasted_iota(jnp.int32, (1, nbuf), 1)
            valid = jnp.bitwise_and(row, dil - 1) == 0
            s_buf = jnp.sum(c_ref[0, 0, h] * q, axis=0, keepdims=True)
            s_buf = jnp.where(valid, s_buf, NEG_INF)
            s_new = jnp.sum(n_ref[0, 0, h] * q, axis=0, keepdims=True)
            m = jnp.maximum(jnp.max(s_buf, axis=1, keepdims=True), s_new)
            p_buf = jnp.exp(s_buf - m)
            p_new = jnp.exp(s_new - m)
            den = jnp.sum(p_buf, axis=1, keepdims=True) + p_new
            o = jnp.sum(c_ref[0, 1, h] * p_buf, axis=1, keepdims=True) + n_ref[0, 1, h] * p_new
            outs.append(o / den)
            lses.append(m + jnp.log(den))
        mx = jnp.maximum(jnp.maximum(lses[0], lses[1]), lses[2])
        e = [jnp.exp(l - mx) for l in lses]
        inv = 1.0 / (e[0] + e[1] + e[2])
        for g in range(N_DIL):
            o_ref[0, g, h] = outs[g] * (e[g] * inv)
        return carry

    lax.fori_loop(0, ATT_HEADS, head, 0)


def _attn_sample(q, new_kvs, caches):
    n_real = q.shape[0]
    assert all(dil & (dil - 1) == 0 and c.shape[-1] == BAND * dil for c, (_, dil) in zip(caches, DIL_PAIRS))
    per_row = lambda a: pl.BlockSpec((1,) + a.shape[1:], lambda b: (b,) + (0,) * (a.ndim - 1))
    return pl.pallas_call(
        _attn_sample_kernel,
        grid=(n_real,),
        in_specs=[per_row(a) for a in (q, *new_kvs, *caches)],
        out_specs=per_row(q),
        out_shape=jax.ShapeDtypeStruct(q.shape, F32),
        compiler_params=_cparams(("arbitrary",)),
        name="attn_sample",
    )(q, *new_kvs, *caches)


def _topk_desc(sc, k):
    vals = []
    cur = sc
    for i in range(k):
        m = jnp.max(cur, axis=0, keepdims=True)
        vals.append(m)
        if i + 1 < k:
            cur = jnp.where(cur == m, NEG_INF, cur)
    return vals


def _outproj_peer_kernel(x_ref, a0_ref, a1_ref, a2_ref, gm_ref, gt1_ref, sh2_ref, sc2_ref, g1_ref, g2_ref,
                         wo_ref, wq_ref, keys_ref,
                         x1_ref, h2t_ref, s1_ref, s2_ref, e2_ref, thr_ref, lse_ref):
    mo = _dot(a0_ref[...], wo_ref[0:GROUP_W, :])
    mo += _dot(a1_ref[...], wo_ref[GROUP_W:2 * GROUP_W, :])
    mo += _dot(a2_ref[...], wo_ref[2 * GROUP_W:3 * GROUP_W, :])
    mo += _dot(gm_ref[...], wo_ref[3 * GROUP_W:4 * GROUP_W, :])
    x1 = x_ref[...] + gt1_ref[...] * _rms(mo, g1_ref[...])
    x1_ref[...] = x1
    h2 = _rms(x1, g2_ref[...]) * (1.0 + sc2_ref[...]) + sh2_ref[...]
    h2t_ref[0] = jnp.transpose(h2).astype(BF16)
    qr = _dot(h2.astype(BF16), wq_ref[...])
    lane_groups = thr_ref.shape[0]
    for h in range(PEER_HEADS):
        tops = []
        for p, s_ref in enumerate((s1_ref, s2_ref)):
            c0 = (h * 2 + p) * PEER_HALF
            sc = _dot_nt(keys_ref[p], qr[:, c0:c0 + PEER_HALF].astype(BF16)) * LOG2E
            for lg in range(lane_groups):
                s_ref[lg, h] = sc[:, lg * LANES:(lg + 1) * LANES]
            tops.append(_topk_desc(sc, PEER_TOPK))
            if p == 1:
                e2 = jnp.exp2(sc - tops[1][0])
                for lg in range(lane_groups):
                    e2_ref[lg, h] = e2[:, lg * LANES:(lg + 1) * LANES]
        t1 = jnp.concatenate(tops[0], axis=0)
        half = PEER_TOPK // 2
        cand = jnp.concatenate(
            [t1 + tops[1][0]]
            + [t1[:half] + tops[1][b] for b in range(1, half)]
            + [jnp.concatenate(tops[1][half:], axis=0) + tops[0][0]], axis=0)
        best = _topk_desc(cand, PEER_TOPK)
        z = jnp.ones_like(best[0])
        for v in best[1:]:
            z = z + jnp.exp2(v - best[0])
        lse = best[0] + jnp.log(z) * LOG2E - tops[1][0]
        for lg in range(lane_groups):
            thr_ref[lg, pl.ds(h, 1), :] = best[-1][:, lg * LANES:(lg + 1) * LANES]
            lse_ref[lg, pl.ds(h, 1), :] = lse[:, lg * LANES:(lg + 1) * LANES]


def _outproj_peer(x, atts, gm, mods, norm_g, w_out, w_q, keys, tm, per_row_mod, tiles_per_seq):
    n = x.shape[0]
    if per_row_mod:
        mod_specs = [pl.BlockSpec((tm, D_MODEL), lambda i: (i, 0))] * 3
    else:
        mod_specs = [pl.BlockSpec((None, 1, D_MODEL), lambda i, k=k: ((i // tiles_per_seq) * 6 + k, 0, 0))
                     for k in (2, 3, 4)]
    row_spec = lambda w: pl.BlockSpec((tm, w), lambda i: (i, 0))
    const = lambda shape: pl.BlockSpec(shape, lambda i: tuple(0 for _ in shape))
    return pl.pallas_call(
        _outproj_peer_kernel,
        grid=(n // tm,),
        in_specs=[row_spec(D_MODEL)] + [row_spec(GROUP_W)] * 4 + mod_specs
        + [pl.BlockSpec((None, 1, D_MODEL), lambda i: (1, 0, 0)),
           pl.BlockSpec((None, 1, D_MODEL), lambda i: (2, 0, 0)),
           const((D_MODEL, D_MODEL)), const((D_MODEL, D_MODEL)), const((2, N_KEYS, PEER_HALF))],
        out_specs=[row_spec(D_MODEL),
                   pl.BlockSpec((1, D_MODEL, tm), lambda i: (i, 0, 0)),
                   pl.BlockSpec((tm // LANES, PEER_HEADS, N_KEYS, LANES), lambda i: (i, 0, 0, 0)),
                   pl.BlockSpec((tm // LANES, PEER_HEADS, N_KEYS, LANES), lambda i: (i, 0, 0, 0)),
                   pl.BlockSpec((tm // LANES, PEER_HEADS, N_KEYS, LANES), lambda i: (i, 0, 0, 0)),
                   pl.BlockSpec((tm // LANES, PEER_HEADS, LANES), lambda i: (i, 0, 0)),
                   pl.BlockSpec((tm // LANES, PEER_HEADS, LANES), lambda i: (i, 0, 0))],
        out_shape=[jax.ShapeDtypeStruct((n, D_MODEL), F32),
                   jax.ShapeDtypeStruct((n // tm, D_MODEL, tm), BF16),
                   jax.ShapeDtypeStruct((n // LANES, PEER_HEADS, N_KEYS, LANES), F32),
                   jax.ShapeDtypeStruct((n // LANES, PEER_HEADS, N_KEYS, LANES), F32),
                   jax.ShapeDtypeStruct((n // LANES, PEER_HEADS, N_KEYS, LANES), F32),
                   jax.ShapeDtypeStruct((n // LANES, PEER_HEADS, LANES), F32),
                   jax.ShapeDtypeStruct((n // LANES, PEER_HEADS, LANES), F32)],
        compiler_params=_cparams(("parallel",)),
        name="outproj_peer",
    )(x, *atts, gm, *mods, norm_g, norm_g, w_out, w_q, keys)


def _expert_kernel(h2t_ref, u_ref, vt_ref, s1_ref, s2_ref, e2_ref, thr_ref, lse_ref, x1_ref, gt2_ref, g3_ref,
                   y_ref, acc_scr, g_scr, w_scr, rowb_scr, thrb_scr, tile_scr):
    j = pl.program_id(1)
    n_tiles = pl.num_programs(1) - 1
    te = u_ref.shape[0]
    rows_per_tile = te // N_KEYS
    lane_groups = s1_ref.shape[0]
    n_chunks, _, chunk_w = h2t_ref.shape
    lg_per_chunk = chunk_w // LANES
    n_rows = rows_per_tile * PEER_HEADS
    ch = 8

    @pl.when(j == 0)
    def _():
        acc_scr[...] = jnp.zeros_like(acc_scr)
        g_scr[...] = jnp.zeros_like(g_scr)
        for lg in range(lane_groups):
            for h in range(PEER_HEADS):
                thrb_scr[lg, h] = jnp.broadcast_to(thr_ref[lg, pl.ds(h, 1), :], (ch, LANES))

    jt = jnp.minimum(j, n_tiles - 1)
    for lg in range(lane_groups):
        for rr in range(rows_per_tile):
            tile = tile_scr.at[lg * rows_per_tile + rr]
            for h in range(PEER_HEADS):
                tile[pl.ds(h, 1), :] = s1_ref[lg, h, pl.ds(jt * rows_per_tile + rr, 1), :]
            rows8 = tile[...]
            e8 = jnp.exp2(rows8 - lse_ref[lg])
            for h in range(PEER_HEADS):
                rowb_scr[lg, rr * PEER_HEADS + h] = jnp.broadcast_to(rows8[h:h + 1], (ch, LANES))
                rowb_scr[lg, n_rows + rr * PEER_HEADS + h] = jnp.broadcast_to(e8[h:h + 1], (ch, LANES))

    def build_gates(lg):
        for cc in range(N_KEYS // ch):
            gates = [None] * rows_per_tile
            for h in range(PEER_HEADS):
                b = s2_ref[lg, h, cc * ch:(cc + 1) * ch, :]
                eb = e2_ref[lg, h, cc * ch:(cc + 1) * ch, :]
                t = thrb_scr[lg, h]
                for rr in range(rows_per_tile):
                    a = rowb_scr[lg, rr * PEER_HEADS + h]
                    ea = rowb_scr[lg, n_rows + rr * PEER_HEADS + h]
                    term = jnp.where(a + b >= t, ea * eb, 0.0)
                    gates[rr] = term if gates[rr] is None else gates[rr] + term
            for rr in range(rows_per_tile):
                g_scr[lg, rr * N_KEYS + cc * ch:rr * N_KEYS + (cc + 1) * ch, :] = gates[rr]

    for k in range(n_chunks):
        act = _dot(u_ref[...], h2t_ref[k])
        for q in range(lg_per_chunk):
            lg = k * lg_per_chunk + q
            w_scr[:, lg * LANES:(lg + 1) * LANES] = (
                g_scr[lg] * _gelu_sigmoid(act[:, q * LANES:(q + 1) * LANES])).astype(BF16)

    d_slab = acc_scr.shape[0] // lane_groups

    def piece(lg, carry):
        d0 = pl.multiple_of(lg * d_slab, d_slab)
        acc_scr[pl.ds(d0, d_slab), :] += _dot(vt_ref[pl.ds(d0, d_slab), :], w_scr[...])
        build_gates(lg)
        return carry

    lax.fori_loop(0, lane_groups, piece, 0)

    @pl.when(j == n_tiles)
    def _():
        peer = jnp.transpose(acc_scr[...])
        y_ref[...] = x1_ref[...] + gt2_ref[...] * _rms(peer, g3_ref[...])


def _experts(h2t, u_bf, vt_bf, s1, s2, e2, thr, lse, x1, gt2, norm_g, tm, per_row_mod, tiles_per_seq):
    chunk_w = h2t.shape[2]
    n = h2t.shape[0] * chunk_w
    te = 512
    n_tiles = N_EXPERTS // te
    prev_tile = lambda j: jnp.maximum(j - 1, 0)
    if per_row_mod:
        gt2_spec = pl.BlockSpec((tm, D_MODEL), lambda i, j: (i, 0))
    else:
        gt2_spec = pl.BlockSpec((None, 1, D_MODEL), lambda i, j: ((i // tiles_per_seq) * 6 + 5, 0, 0))
    return pl.pallas_call(
        _expert_kernel,
        grid=(n // tm, n_tiles + 1),
        in_specs=[
            pl.BlockSpec((tm // chunk_w, D_MODEL, chunk_w), lambda i, j: (i, 0, 0)),
            pl.BlockSpec((te, D_MODEL), lambda i, j: (prev_tile(j), 0)),
            pl.BlockSpec((D_MODEL, te), lambda i, j: (0, prev_tile(j))),
            pl.BlockSpec((tm // LANES, PEER_HEADS, N_KEYS, LANES), lambda i, j: (i, 0, 0, 0)),
            pl.BlockSpec((tm // LANES, PEER_HEADS, N_KEYS, LANES), lambda i, j: (i, 0, 0, 0)),
            pl.BlockSpec((tm // LANES, PEER_HEADS, N_KEYS, LANES), lambda i, j: (i, 0, 0, 0)),
            pl.BlockSpec((tm // LANES, PEER_HEADS, LANES), lambda i, j: (i, 0, 0)),
            pl.BlockSpec((tm // LANES, PEER_HEADS, LANES), lambda i, j: (i, 0, 0)),
            pl.BlockSpec((tm, D_MODEL), lambda i, j: (i, 0)),
            gt2_spec,
            pl.BlockSpec((None, 1, D_MODEL), lambda i, j: (3, 0, 0)),
        ],
        out_specs=pl.BlockSpec((tm, D_MODEL), lambda i, j: (i, 0)),
        out_shape=jax.ShapeDtypeStruct((n, D_MODEL), F32),
        scratch_shapes=[pltpu.VMEM((D_MODEL, tm), F32), pltpu.VMEM((tm // LANES, te, LANES), F32),
                        pltpu.VMEM((te, tm), BF16),
                        pltpu.VMEM((tm // LANES, 2 * (te // N_KEYS) * PEER_HEADS, 8, LANES), F32),
                        pltpu.VMEM((tm // LANES, PEER_HEADS, 8, LANES), F32),
                        pltpu.VMEM((tm // LANES * (te // N_KEYS), PEER_HEADS, LANES), F32)],
        compiler_params=_cparams(("parallel", "arbitrary")),
        name="peer_experts",
    )(h2t, u_bf, vt_bf, s1, s2, e2, thr, lse, x1, gt2, norm_g)


def _permute_w_in(w_in):
    q = w_in[:, :ATT_WIDTH]
    k = w_in[:, ATT_WIDTH:2 * ATT_WIDTH]
    v = w_in[:, 2 * ATT_WIDTH:3 * ATT_WIDTH]
    parts = [q]
    for g in range(N_DIL):
        parts += [k[:, g * GROUP_W:(g + 1) * GROUP_W], v[:, g * GROUP_W:(g + 1) * GROUP_W]]
    parts.append(w_in[:, 3 * ATT_WIDTH:])
    return jnp.concatenate(parts, axis=1).astype(BF16)


def kernel(x_prompt, x_sample, c_prompt, c_sample, cache_kv_dil1, cache_kv_dil4, cache_kv_dil16, w_ada, b_ada, norm_g, w_in, gm_ln_g, gm_ln_b, w_s, b_s, w_out, w_q, sub_keys, expert_u, expert_v):
    batch, seq, _ = x_prompt.shape
    n_s = x_sample.shape[0] * x_sample.shape[1]
    assert w_ada.shape[0] == 1 and x_sample.shape[1] == 1
    n_p = batch * seq
    pad_s = 128

    w_in_p = _permute_w_in(w_in[0])
    w_out_b = w_out[0].astype(BF16)
    w_q_b = w_q[0].astype(BF16)
    keys_b = sub_keys[0].astype(BF16)
    u_b = expert_u[0].astype(BF16)
    vt_b = jnp.transpose(expert_v[0].astype(BF16))
    ng = norm_g[0].reshape(4, 1, D_MODEL)
    ln_g, ln_b = gm_ln_g[0], gm_ln_b[0]

    rows = batch + n_s
    c_all = jnp.zeros((16, D_MODEL), F32).at[:rows].set(jnp.concatenate([c_prompt, c_sample], axis=0))
    mod = _ada(c_all, w_ada[0], b_ada)
    mod_p = mod.reshape(16 * 6, 1, D_MODEL)
    mod_s = jnp.zeros((pad_s, 6, D_MODEL), F32).at[:n_s].set(mod[batch:rows].reshape(n_s, 6, D_MODEL))
    mods_s = [mod_s[:, k] for k in range(6)]

    xp = x_prompt.reshape(n_p, D_MODEL)
    bsb = jnp.broadcast_to(b_s[0][:, :, None], (GM_HEADS, CHUNK, GM_HD))
    outs = _inproj_prompt(xp, mod_p, ng, w_in_p, w_s[0], bsb, ln_g, ln_b, batch, seq)
    folded, kv_p, gm_p = outs[:9], outs[9:12], outs[12]
    att_p = _attn_prompt(folded, batch, seq)
    tm_p = 256
    x1_p, h2t_p, *sel_p = _outproj_peer(
        xp, att_p, gm_p, [mod_p] * 3, ng, w_out_b, w_q_b, keys_b, tm_p, False, seq // tm_p)
    tm_e = 512
    y_p = _experts(h2t_p, u_b, vt_b, *sel_p, x1_p, mod_p, ng, tm_e, False, seq // tm_e)

    xs = jnp.zeros((pad_s, D_MODEL), F32).at[:n_s].set(x_sample.reshape(n_s, D_MODEL))
    wd = jnp.repeat(w_s[0][:, 0, 0], GM_HD)[None, :]
    b0 = jnp.repeat(b_s[0][:, 0], GM_HD)[None, :]
    q_s, kv0_s, kv1_s, kv2_s, gm_s, gvn_s = _inproj_sample(
        xs, mods_s[0], mods_s[1], ng, w_in_p, wd, b0, ln_g, ln_b)
    new_kv_s = [kv[:n_s].reshape(n_s, 2, ATT_HEADS, ATT_HD) for kv in (kv0_s, kv1_s, kv2_s)]
    caches_t = [jnp.transpose(c[0], (0, 2, 3, 4, 1)) for c in (cache_kv_dil1, cache_kv_dil4, cache_kv_dil16)]
    att_s = _attn_sample(q_s[:n_s].reshape(n_s, N_DIL, ATT_HEADS, ATT_HD, 1),
                         [kv[..., None] for kv in new_kv_s], caches_t)
    att_s = att_s.reshape(n_s, N_DIL, GROUP_W).astype(BF16)
    att_s = [jnp.zeros((pad_s, GROUP_W), BF16).at[:n_s].set(att_s[:, g]) for g in range(N_DIL)]
    x1_s, h2t_s, *sel_s = _outproj_peer(
        xs, att_s, gm_s, mods_s[2:5], ng, w_out_b, w_q_b, keys_b, pad_s, True, 1)
    y_s = _experts(h2t_s, u_b, vt_b, *sel_s, x1_s, mods_s[5], ng, pad_s, True, 1)

    y_prompt = y_p.reshape(batch, seq, D_MODEL)
    y_sample = y_s[:n_s].reshape(x_sample.shape)
    new_kv_p = []
    for kv_t, (win, _) in zip(kv_p, DIL_PAIRS):
        keep = min(win, seq)
        kv_t = kv_t[:, :, seq - keep:].reshape(batch, 2, ATT_HEADS, ATT_HD, keep)
        new_kv_p.append(jnp.transpose(kv_t, (0, 4, 1, 2, 3))[None])
    new_kv_s = [kv.reshape(1, n_s, 1, 2, ATT_HEADS, ATT_HD) for kv in new_kv_s]
    state_gv = gvn_s[:n_s].reshape(1, n_s, 1, GM_HEADS, GM_HD)
    return (y_prompt, y_sample, *new_kv_p, *new_kv_s, state_gv)
```

```python
import functools
import math

import jax
import jax.numpy as jnp
from jax import lax
from jax.experimental import pallas as pl
from jax.experimental.pallas import tpu as pltpu

F32 = jnp.float32
BF16 = jnp.bfloat16

D_MODEL = 2048
ATT_HD = 64
ATT_HEADS = 8
GROUP_W = ATT_HEADS * ATT_HD
DIL_PAIRS = ((128, 1), (512, 4), (2048, 16))
N_DIL = 3
ATT_WIDTH = N_DIL * GROUP_W
QBLK = 128
BAND = 128
CHUNK = 128
GM_HEADS = 4
GM_HD = 128
GM_WIDTH = GM_HEADS * GM_HD
IN_WIDTH = 3 * ATT_WIDTH + 2 * GM_WIDTH
N_KEYS = 128
N_EXPERTS = N_KEYS * N_KEYS
PEER_HEADS = 8
PEER_HALF = 128
PEER_TOPK = 16
EPS = 1e-6
NEG_INF = float("-inf")
LOG2E = 1.4426950408889634

LANES = 128
VMEM_LIMIT = 56 * 1024 * 1024

Q_OFF = 0
KV_OFF = ATT_WIDTH
GU_OFF = ATT_WIDTH + N_DIL * 2 * GROUP_W
GV_OFF = GU_OFF + GM_WIDTH


def _cparams(sem, flags=None):
    return pltpu.CompilerParams(dimension_semantics=sem, vmem_limit_bytes=VMEM_LIMIT, flags=flags)


def _gelu(x):
    c = math.sqrt(2.0 / math.pi)
    return 0.5 * x * (1.0 + jnp.tanh(c * (x + 0.044715 * (x * x * x))))


def _gelu_sigmoid(x):
    k2 = -2.0 * math.sqrt(2.0 / math.pi) * LOG2E
    k1 = k2 * 0.044715
    return x / (1.0 + jnp.exp2(x * (k2 + k1 * (x * x))))


def _rms(x, g):
    return x * lax.rsqrt(jnp.mean(x * x, axis=-1, keepdims=True) + EPS) * g


def _dot(a, b):
    return jnp.dot(a, b, preferred_element_type=F32)


def _dot_nt(a, b):
    return lax.dot_general(a, b, (((1,), (1,)), ((), ())), preferred_element_type=F32)


def _ada_kernel(c_ref, w_ref, b_ref, o_ref):
    c = c_ref[...]
    act = (c / (1.0 + jnp.exp(-c))).astype(BF16)
    o_ref[...] = _dot(act, w_ref[...].astype(BF16)) + b_ref[...]


def _ada(c_all, w_ada, b_ada):
    rows = c_all.shape[0]
    n_out = w_ada.shape[1]
    tn = 1024
    return pl.pallas_call(
        _ada_kernel,
        grid=(n_out // tn,),
        in_specs=[
            pl.BlockSpec((rows, D_MODEL), lambda j: (0, 0)),
            pl.BlockSpec((D_MODEL, tn), lambda j: (0, j)),
            pl.BlockSpec((1, tn), lambda j: (0, j)),
        ],
        out_specs=pl.BlockSpec((rows, tn), lambda j: (0, j)),
        out_shape=jax.ShapeDtypeStruct((rows, n_out), F32),
        compiler_params=_cparams(("parallel",)),
        name="ada_mod",
    )(c_all, w_ada, b_ada)


def _modulated_norm(x_ref, sh_ref, sc_ref, g_ref):
    x = x_ref[...]
    return _rms(x, g_ref[...]) * (1.0 + sc_ref[...]) + sh_ref[...]


def _gm_layernorm(gv, lng_ref, lnb_ref, hh):
    v = gv[:, hh * GM_HD:(hh + 1) * GM_HD]
    mu = jnp.mean(v, axis=-1, keepdims=True)
    var = jnp.mean(jnp.square(v - mu), axis=-1, keepdims=True)
    return (v - mu) * lax.rsqrt(var + EPS) * lng_ref[pl.ds(hh, 1), :] + lnb_ref[pl.ds(hh, 1), :]


def _inproj_prompt_kernel(x_ref, sh_ref, sc_ref, g_ref, w_ref, ws_ref, bsb_ref, lng_ref, lnb_ref,
                          q0_ref, k0_ref, v0_ref, q1_ref, k1_ref, v1_ref, q2_ref, k2_ref, v2_ref,
                          kv0_ref, kv1_ref, kv2_ref, gm_ref, h_scr, y_scr):
    tm = x_ref.shape[0]
    h_scr[...] = _modulated_norm(x_ref, sh_ref, sc_ref, g_ref).astype(BF16)

    def proj(off):
        return _dot(h_scr[...], w_ref[:, off:off + GROUP_W])

    folded = ((q0_ref, k0_ref, v0_ref), (q1_ref, k1_ref, v1_ref), (q2_ref, k2_ref, v2_ref))
    kv_refs = (kv0_ref, kv1_ref, kv2_ref)
    for g, (_, dil) in enumerate(DIL_PAIRS):
        offs = (Q_OFF + g * GROUP_W, KV_OFF + g * 2 * GROUP_W, KV_OFF + g * 2 * GROUP_W + GROUP_W)
        for which, off in enumerate(offs):
            y = proj(off)
            if which > 0:
                kv_refs[g][0, (which - 1) * GROUP_W:which * GROUP_W, :] = jnp.transpose(y)
            else:
                y = y * (ATT_HD ** -0.5)
            dst = folded[g][which]
            if dil == 1:
                dst[...] = y.astype(BF16)
            else:
                for cb in range(GROUP_W // LANES):
                    y_scr[cb] = y[:, cb * LANES:(cb + 1) * LANES]
                for r in range(dil):
                    for cb in range(GROUP_W // LANES):
                        dst[0, r, :, cb * LANES:(cb + 1) * LANES] = (
                            y_scr[cb, pl.ds(r, tm // dil, stride=dil), :].astype(BF16))

    gu = _gelu(proj(GU_OFF))
    gv = _gelu(proj(GV_OFF))
    row = lax.broadcasted_iota(jnp.int32, (CHUNK, CHUNK), 0)
    col = lax.broadcasted_iota(jnp.int32, (CHUNK, CHUNK), 1)
    for hh in range(GM_HEADS):
        gvn = _gm_layernorm(gv, lng_ref, lnb_ref, hh).astype(BF16)
        w_tril = jnp.where(row >= col, ws_ref[hh], 0.0).astype(BF16)
        for ci in range(tm // CHUNK):
            z = _dot(w_tril, gvn[ci * CHUNK:(ci + 1) * CHUNK]) + bsb_ref[hh]
            gu_blk = gu[ci * CHUNK:(ci + 1) * CHUNK, hh * GM_HD:(hh + 1) * GM_HD]
            gm_ref[ci * CHUNK:(ci + 1) * CHUNK, hh * GM_HD:(hh + 1) * GM_HD] = (gu_blk * z).astype(BF16)


def _inproj_prompt(x, mod, norm_g, w_in_p, w_s, bsb, ln_g, ln_b, batch, seq):
    n = x.shape[0]
    tm = 256
    tiles_per_seq = seq // tm

    def mod_spec(k):
        return pl.BlockSpec((None, 1, D_MODEL), lambda i: ((i // tiles_per_seq) * 6 + k, 0, 0))

    def folded_spec(dil):
        if dil == 1:
            return pl.BlockSpec((tm, GROUP_W), lambda i: (i, 0))
        return pl.BlockSpec((1, dil, tm // dil, GROUP_W),
                            lambda i: (i // tiles_per_seq, 0, i % tiles_per_seq, 0))

    def folded_shape(dil):
        if dil == 1:
            return jax.ShapeDtypeStruct((n, GROUP_W), BF16)
        return jax.ShapeDtypeStruct((batch, dil, seq // dil, GROUP_W), BF16)

    out_specs, out_shape = [], []
    for _, dil in DIL_PAIRS:
        for _ in range(3):
            out_specs.append(folded_spec(dil))
            out_shape.append(folded_shape(dil))
    for _ in range(N_DIL):
        out_specs.append(pl.BlockSpec((1, 2 * GROUP_W, tm),
                                      lambda i: (i // tiles_per_seq, 0, i % tiles_per_seq)))
        out_shape.append(jax.ShapeDtypeStruct((batch, 2 * GROUP_W, seq), F32))
    out_specs.append(pl.BlockSpec((tm, GM_WIDTH), lambda i: (i, 0)))
    out_shape.append(jax.ShapeDtypeStruct((n, GM_WIDTH), BF16))

    return pl.pallas_call(
        _inproj_prompt_kernel,
        grid=(n // tm,),
        in_specs=[
            pl.BlockSpec((tm, D_MODEL), lambda i: (i, 0)),
            mod_spec(0), mod_spec(1),
            pl.BlockSpec((None, 1, D_MODEL), lambda i: (0, 0, 0)),
            pl.BlockSpec((D_MODEL, IN_WIDTH), lambda i: (0, 0)),
            pl.BlockSpec((GM_HEADS, CHUNK, CHUNK), lambda i: (0, 0, 0)),
            pl.BlockSpec((GM_HEADS, CHUNK, GM_HD), lambda i: (0, 0, 0)),
            pl.BlockSpec((GM_HEADS, GM_HD), lambda i: (0, 0)),
            pl.BlockSpec((GM_HEADS, GM_HD), lambda i: (0, 0)),
        ],
        out_specs=out_specs,
        out_shape=out_shape,
        scratch_shapes=[pltpu.VMEM((tm, D_MODEL), BF16), pltpu.VMEM((GROUP_W // LANES, tm, LANES), F32)],
        compiler_params=_cparams(("parallel",)),
        name="inproj_prompt",
    )(x, mod, mod, norm_g, w_in_p, w_s, bsb, ln_g, ln_b)


def _inproj_sample_kernel(x_ref, sh_ref, sc_ref, g_ref, w_ref, wd_ref, b0_ref, lng_ref, lnb_ref,
                          q_ref, kv0_ref, kv1_ref, kv2_ref, gm_ref, gvn_ref):
    h = _modulated_norm(x_ref, sh_ref, sc_ref, g_ref).astype(BF16)
    q_ref[...] = _dot(h, w_ref[:, Q_OFF:Q_OFF + ATT_WIDTH])
    for g, kv_ref in enumerate((kv0_ref, kv1_ref, kv2_ref)):
        off = KV_OFF + g * 2 * GROUP_W
        kv_ref[...] = _dot(h, w_ref[:, off:off + 2 * GROUP_W])
    gu = _gelu(_dot(h, w_ref[:, GU_OFF:GU_OFF + GM_WIDTH]))
    gv = _gelu(_dot(h, w_ref[:, GV_OFF:GV_OFF + GM_WIDTH]))
    for hh in range(GM_HEADS):
        sl = slice(hh * GM_HD, (hh + 1) * GM_HD)
        gvn = _gm_layernorm(gv, lng_ref, lnb_ref, hh)
        gvn_ref[:, sl] = gvn
        z = wd_ref[:, sl] * gvn + b0_ref[:, sl]
        gm_ref[:, sl] = (gu[:, sl] * z).astype(BF16)


def _inproj_sample(x, sh, sc, norm_g, w_in_p, wd, b0, ln_g, ln_b):
    n = x.shape[0]
    full = lambda shape: pl.BlockSpec(shape, lambda i: tuple(0 for _ in shape))
    return pl.pallas_call(
        _inproj_sample_kernel,
        grid=(1,),
        in_specs=[
            full((n, D_MODEL)), full((n, D_MODEL)), full((n, D_MODEL)),
            pl.BlockSpec((None, 1, D_MODEL), lambda i: (0, 0, 0)),
            full((D_MODEL, IN_WIDTH)),
            full((1, GM_WIDTH)), full((1, GM_WIDTH)),
            full((GM_HEADS, GM_HD)), full((GM_HEADS, GM_HD)),
        ],
        out_specs=[full((n, ATT_WIDTH)), full((n, 2 * GROUP_W)), full((n, 2 * GROUP_W)),
                   full((n, 2 * GROUP_W)), full((n, GM_WIDTH)), full((n, GM_WIDTH))],
        out_shape=[jax.ShapeDtypeStruct((n, ATT_WIDTH), F32)]
        + [jax.ShapeDtypeStruct((n, 2 * GROUP_W), F32)] * 3
        + [jax.ShapeDtypeStruct((n, GM_WIDTH), BF16), jax.ShapeDtypeStruct((n, GM_WIDTH), F32)],
        compiler_params=_cparams(("arbitrary",)),
        name="inproj_sample",
    )(x, sh, sc, norm_g, w_in_p, wd, b0, ln_g, ln_b)


def _attn_block(qt, kt, vt, off, nkeys):
    lane = lax.broadcasted_iota(jnp.int32, (1, LANES), 1)
    lo = lane < ATT_HD
    qi = lax.broadcasted_iota(jnp.int32, (QBLK, nkeys), 0)
    kk = lax.broadcasted_iota(jnp.int32, (QBLK, nkeys), 1)
    dist = qi - kk + off
    valid = lax.bitcast_convert_type(dist, jnp.uint32) <= jnp.uint32(BAND)
    res = []
    for sel in (lo, jnp.logical_not(lo)):
        qm = jnp.where(sel, qt, jnp.zeros_like(qt))
        s = jnp.where(valid, _dot_nt(qm, kt), NEG_INF)
        m = jnp.max(s, axis=-1, keepdims=True)
        p = jnp.exp(s - m)
        den = jnp.sum(p, axis=-1, keepdims=True)
        o = _dot(p.astype(BF16), vt) * (1.0 / den)
        res.append((o, m + jnp.log(den)))
    o = jnp.where(lo, res[0][0], res[1][0])
    lse = jnp.where(lo, res[0][1], res[1][1])
    return o, lse


def _attn_prompt_kernel(q0_ref, k0_ref, v0_ref, q1_ref, k1_ref, v1_ref, q2_ref, k2_ref, v2_ref,
                        a0_ref, a1_ref, a2_ref, of_scr, lf_scr, to_scr, tl_scr):
    seq = q0_ref.shape[0]

    def run_block(g, qt, kt, vt, off, nkeys, row0):
        o, lse = _attn_block(qt, kt, vt, off, nkeys)
        of_scr[g, pl.ds(row0, QBLK), :] = o
        lf_scr[g, pl.ds(row0, QBLK), :] = lse

    par = 8

    def g0_body(it, carry):
        for u in range(par):
            qb = it * par + u
            qs = pl.multiple_of(qb * QBLK, QBLK)
            ks = pl.multiple_of(jnp.maximum(qb - 1, 0) * QBLK, QBLK)
            run_block(0, q0_ref[pl.ds(qs, QBLK), :], k0_ref[pl.ds(ks, 2 * QBLK), :],
                      v0_ref[pl.ds(ks, 2 * QBLK), :], qs - ks, 2 * QBLK, qs)
        return carry

    lax.fori_loop(0, seq // QBLK // par, g0_body, 0)

    dil1 = DIL_PAIRS[1][1]
    len1 = seq // dil1

    res_per_body = max(par // (len1 // QBLK), 1)

    def g1_body(it, carry):
        for u in range(res_per_body):
            r = it * res_per_body + u
            for qb in range(len1 // QBLK):
                qs = qb * QBLK
                ks = max(qb - 1, 0) * QBLK
                run_block(1, q1_ref[0, r, pl.ds(qs, QBLK), :], k1_ref[0, r, pl.ds(ks, 2 * QBLK), :],
                          v1_ref[0, r, pl.ds(ks, 2 * QBLK), :], qs - ks, 2 * QBLK,
                          pl.multiple_of(r * len1 + qs, QBLK))
        return carry

    lax.fori_loop(0, dil1 // res_per_body, g1_body, 0)

    dil2 = DIL_PAIRS[2][1]
    len2 = seq // dil2

    def g2_body(it, carry):
        for u in range(par):
            r = it * par + u
            pad = it * par + (u ^ 1)
            kt = jnp.concatenate([k2_ref[0, r], k2_ref[0, pad]], axis=0)
            vt = jnp.concatenate([v2_ref[0, r], v2_ref[0, pad]], axis=0)
            run_block(2, q2_ref[0, r], kt, vt, 0, 2 * QBLK, pl.multiple_of(r * len2, QBLK))
        return carry

    lax.fori_loop(0, dil2 // par, g2_body, 0)

    def mix_body(nt, carry):
        t0 = pl.multiple_of(nt * QBLK, QBLK)
        o_nat = [of_scr[0, pl.ds(t0, QBLK), :]]
        l_nat = [lf_scr[0, pl.ds(t0, QBLK), :]]
        for g, dil, length in ((1, dil1, len1), (2, dil2, len2)):
            per = QBLK // dil
            for r in range(dil):
                src = pl.multiple_of(r * length + nt * per, per)
                to_scr[g - 1, pl.ds(r, per, stride=dil), :] = of_scr[g, pl.ds(src, per), :]
                tl_scr[g - 1, pl.ds(r, per, stride=dil), :] = lf_scr[g, pl.ds(src, per), :]
            o_nat.append(to_scr[g - 1])
            l_nat.append(tl_scr[g - 1])
        mx = jnp.maximum(jnp.maximum(l_nat[0], l_nat[1]), l_nat[2])
        e = [jnp.exp(l - mx) for l in l_nat]
        inv = 1.0 / (e[0] + e[1] + e[2])
        for g, a_ref in enumerate((a0_ref, a1_ref, a2_ref)):
            a_ref[pl.ds(t0, QBLK), :] = (o_nat[g] * (e[g] * inv)).astype(BF16)
        return carry

    lax.fori_loop(0, seq // QBLK, mix_body, 0)


def _attn_prompt(folded, batch, seq):
    n = batch * seq
    pairs = GROUP_W // LANES
    in_specs = []
    for _, dil in DIL_PAIRS:
        for _ in range(3):
            if dil == 1:
                in_specs.append(pl.BlockSpec((seq, LANES), lambda b, hp: (b, hp)))
            else:
                in_specs.append(pl.BlockSpec((1, dil, seq // dil, LANES), lambda b, hp: (b, 0, 0, hp)))
    return pl.pallas_call(
        _attn_prompt_kernel,
        grid=(batch, pairs),
        in_specs=in_specs,
        out_specs=[pl.BlockSpec((seq, LANES), lambda b, hp: (b, hp))] * N_DIL,
        out_shape=[jax.ShapeDtypeStruct((n, GROUP_W), BF16)] * N_DIL,
        scratch_shapes=[pltpu.VMEM((N_DIL, seq, LANES), F32), pltpu.VMEM((N_DIL, seq, LANES), F32),
                        pltpu.VMEM((N_DIL - 1, QBLK, LANES), F32), pltpu.VMEM((N_DIL - 1, QBLK, LANES), F32)],
        compiler_params=_cparams(("parallel", "parallel")),
        name="attn_prompt",
    )(*folded)


def _attn_sample_kernel(q_ref, n0_ref, n1_ref, n2_ref, c0_ref, c1_ref, c2_ref, o_ref):
    groups = ((n0_ref, c0_ref), (n1_ref, c1_ref), (n2_ref, c2_ref))

    def head(h, carry):
        outs, lses = [], []
        for g, (n_ref, c_ref) in enumerate(groups):
            dil = DIL_PAIRS[g][1]
            nbuf = c_ref.shape[-1]
            q = q_ref[0, g, h] * (ATT_HD ** -0.5)
            row = lax.broadcasted_iota(jnp.int32, (1, nbuf), 1)
            valid = jnp.bitwise_and(row, dil - 1) == 0
            s_buf = jnp.sum(c_ref[0, 0, h] * q, axis=0, keepdims=True)
            s_buf = jnp.where(valid, s_buf, NEG_INF)
            s_new = jnp.sum(n_ref[0, 0, h] * q, axis=0, keepdims=True)
            m = jnp.maximum(jnp.max(s_buf, axis=1, keepdims=True), s_new)
            p_buf = jnp.exp(s_buf - m)
            p_new = jnp.exp(s_new - m)
            den = jnp.sum(p_buf, axis=1, keepdims=True) + p_new
            o = jnp.sum(c_ref[0, 1, h] * p_buf, axis=1, keepdims=True) + n_ref[0, 1, h] * p_new
            outs.append(o / den)
            lses.append(m + jnp.log(den))
        mx = jnp.maximum(jnp.maximum(lses[0], lses[1]), lses[2])
        e = [jnp.exp(l - mx) for l in lses]
        inv = 1.0 / (e[0] + e[1] + e[2])
        for g in range(N_DIL):
            o_ref[0, g, h] = outs[g] * (e[g] * inv)
        return carry

    lax.fori_loop(0, ATT_HEADS, head, 0)


def _attn_sample(q, new_kvs, caches):
    n_real = q.shape[0]
    assert all(dil & (dil - 1) == 0 and c.shape[-1] == BAND * dil for c, (_, dil) in zip(caches, DIL_PAIRS))
    per_row = lambda a: pl.BlockSpec((1,) + a.shape[1:], lambda b: (b,) + (0,) * (a.ndim - 1))
    return pl.pallas_call(
        _attn_sample_kernel,
        grid=(n_real,),
        in_specs=[per_row(a) for a in (q, *new_kvs, *caches)],
        out_specs=per_row(q),
        out_shape=jax.ShapeDtypeStruct(q.shape, F32),
        compiler_params=_cparams(("arbitrary",)),
        name="attn_sample",
    )(q, *new_kvs, *caches)


SUBLANES = 8


def _oddeven_merge_sort_pairs(n):
    pairs = []
    p = 1
    while p < n:
        k = p
        while k >= 1:
            for j in range(k % p, n - k, 2 * k):
                for i in range(min(k, n - j - k)):
                    if (i + j) // (2 * p) == (i + j + k) // (2 * p):
                        pairs.append((i + j, i + j + k))
            k //= 2
        p *= 2
    return pairs


def _topk_desc(sc, k):
    n = sc.shape[0] // SUBLANES
    rows = [sc[SUBLANES * i:SUBLANES * (i + 1)] for i in range(n)]
    for i, j in _oddeven_merge_sort_pairs(n):
        rows[i], rows[j] = jnp.maximum(rows[i], rows[j]), jnp.minimum(rows[i], rows[j])
    vals = []
    for it in range(k):
        m = jnp.max(rows[0], axis=0, keepdims=True)
        vals.append(m)
        if it + 1 < k:
            win = rows[0] == m
            depth = min(n, k - it)
            for r in range(depth - 1):
                rows[r] = jnp.where(win, rows[r + 1], rows[r])
            rows[depth - 1] = jnp.where(win, NEG_INF, rows[depth - 1])
    return vals


def _outproj_peer_kernel(x_ref, a0_ref, a1_ref, a2_ref, gm_ref, gt1_ref, sh2_ref, sc2_ref, g1_ref, g2_ref,
                         wo_ref, wq_ref, keys_ref,
                         x1_ref, h2t_ref, s1_ref, s2_ref, e2_ref, thr_ref, lse_ref):
    mo = _dot(a0_ref[...], wo_ref[0:GROUP_W, :])
    mo += _dot(a1_ref[...], wo_ref[GROUP_W:2 * GROUP_W, :])
    mo += _dot(a2_ref[...], wo_ref[2 * GROUP_W:3 * GROUP_W, :])
    mo += _dot(gm_ref[...], wo_ref[3 * GROUP_W:4 * GROUP_W, :])
    x1 = x_ref[...] + gt1_ref[...] * _rms(mo, g1_ref[...])
    x1_ref[...] = x1
    h2 = _rms(x1, g2_ref[...]) * (1.0 + sc2_ref[...]) + sh2_ref[...]
    h2t_ref[0] = jnp.transpose(h2).astype(BF16)
    qr = _dot(h2.astype(BF16), wq_ref[...])
    lane_groups = thr_ref.shape[0]
    for h in range(PEER_HEADS):
        tops = []
        for p, s_ref in enumerate((s1_ref, s2_ref)):
            c0 = (h * 2 + p) * PEER_HALF
            sc = _dot_nt(keys_ref[p], qr[:, c0:c0 + PEER_HALF].astype(BF16)) * LOG2E
            for lg in range(lane_groups):
                s_ref[lg, h] = sc[:, lg * LANES:(lg + 1) * LANES]
            tops.append(_topk_desc(sc, PEER_TOPK))
            if p == 1:
                e2 = jnp.exp2(sc - tops[1][0])
                for lg in range(lane_groups):
                    e2_ref[lg, h] = e2[:, lg * LANES:(lg + 1) * LANES]
        t1 = jnp.concatenate(tops[0], axis=0)
        half = PEER_TOPK // 2
        cand = jnp.concatenate(
            [t1 + tops[1][0]]
            + [t1[:half] + tops[1][b] for b in range(1, half)]
            + [jnp.concatenate(tops[1][half:], axis=0) + tops[0][0]], axis=0)
        best = _topk_desc(cand, PEER_TOPK)
        z = jnp.ones_like(best[0])
        for v in best[1:]:
            z = z + jnp.exp2(v - best[0])
        lse = best[0] + jnp.log(z) * LOG2E - tops[1][0]
        for lg in range(lane_groups):
            thr_ref[lg, pl.ds(h, 1), :] = best[-1][:, lg * LANES:(lg + 1) * LANES]
            lse_ref[lg, pl.ds(h, 1), :] = lse[:, lg * LANES:(lg + 1) * LANES]


def _outproj_peer(x, atts, gm, mods, norm_g, w_out, w_q, keys, tm, per_row_mod, tiles_per_seq):
    n = x.shape[0]
    if per_row_mod:
        mod_specs = [pl.BlockSpec((tm, D_MODEL), lambda i: (i, 0))] * 3
    else:
        mod_specs = [pl.BlockSpec((None, 1, D_MODEL), lambda i, k=k: ((i // tiles_per_seq) * 6 + k, 0, 0))
                     for k in (2, 3, 4)]
    row_spec = lambda w: pl.BlockSpec((tm, w), lambda i: (i, 0))
    const = lambda shape: pl.BlockSpec(shape, lambda i: tuple(0 for _ in shape))
    return pl.pallas_call(
        _outproj_peer_kernel,
        grid=(n // tm,),
        in_specs=[row_spec(D_MODEL)] + [row_spec(GROUP_W)] * 4 + mod_specs
        + [pl.BlockSpec((None, 1, D_MODEL), lambda i: (1, 0, 0)),
           pl.BlockSpec((None, 1, D_MODEL), lambda i: (2, 0, 0)),
           const((D_MODEL, D_MODEL)), const((D_MODEL, D_MODEL)),
           const((2, N_KEYS, PEER_HALF))],
        out_specs=[row_spec(D_MODEL),
                   pl.BlockSpec((1, D_MODEL, tm), lambda i: (i, 0, 0)),
                   pl.BlockSpec((tm // LANES, PEER_HEADS, N_KEYS, LANES), lambda i: (i, 0, 0, 0)),
                   pl.BlockSpec((tm // LANES, PEER_HEADS, N_KEYS, LANES), lambda i: (i, 0, 0, 0)),
                   pl.BlockSpec((tm // LANES, PEER_HEADS, N_KEYS, LANES), lambda i: (i, 0, 0, 0)),
                   pl.BlockSpec((tm // LANES, PEER_HEADS, LANES), lambda i: (i, 0, 0)),
                   pl.BlockSpec((tm // LANES, PEER_HEADS, LANES), lambda i: (i, 0, 0))],
        out_shape=[jax.ShapeDtypeStruct((n, D_MODEL), F32),
                   jax.ShapeDtypeStruct((n // tm, D_MODEL, tm), BF16),
                   jax.ShapeDtypeStruct((n // LANES, PEER_HEADS, N_KEYS, LANES), F32),
                   jax.ShapeDtypeStruct((n // LANES, PEER_HEADS, N_KEYS, LANES), F32),
                   jax.ShapeDtypeStruct((n // LANES, PEER_HEADS, N_KEYS, LANES), F32),
                   jax.ShapeDtypeStruct((n // LANES, PEER_HEADS, LANES), F32),
                   jax.ShapeDtypeStruct((n // LANES, PEER_HEADS, LANES), F32)],
        compiler_params=_cparams(("parallel",)),
        name="outproj_peer",
    )(x, *atts, gm, *mods, norm_g, norm_g, w_out, w_q, keys)


def _expert_kernel(h2t_ref, u_ref, vt_ref, s1_ref, s2_ref, e2_ref, thr_ref, lse_ref, x1_ref, gt2_ref, g3_ref,
                   y_ref, acc_scr, g_scr, w_scr, rowb_scr, thrb_scr, tile_scr):
    j = pl.program_id(1)
    n_tiles = pl.num_programs(1) - 1
    te = u_ref.shape[0]
    rows_per_tile = te // N_KEYS
    lane_groups = s1_ref.shape[0]
    n_chunks, _, chunk_w = h2t_ref.shape
    lg_per_chunk = chunk_w // LANES
    n_rows = rows_per_tile * PEER_HEADS
    ch = 8

    @pl.when(j == 0)
    def _():
        acc_scr[...] = jnp.zeros_like(acc_scr)
        g_scr[...] = jnp.zeros_like(g_scr)
        for lg in range(lane_groups):
            for h in range(PEER_HEADS):
                thrb_scr[lg, h] = jnp.broadcast_to(thr_ref[lg, pl.ds(h, 1), :], (ch, LANES))

    jt = jnp.minimum(j, n_tiles - 1)
    for lg in range(lane_groups):
        for rr in range(rows_per_tile):
            tile = tile_scr.at[lg * rows_per_tile + rr]
            for h in range(PEER_HEADS):
                tile[pl.ds(h, 1), :] = s1_ref[lg, h, pl.ds(jt * rows_per_tile + rr, 1), :]
            rows8 = tile[...]
            e8 = jnp.exp2(rows8 - lse_ref[lg])
            for h in range(PEER_HEADS):
                rowb_scr[lg, rr * PEER_HEADS + h] = jnp.broadcast_to(rows8[h:h + 1], (ch, LANES))
                rowb_scr[lg, n_rows + rr * PEER_HEADS + h] = jnp.broadcast_to(e8[h:h + 1], (ch, LANES))

    def build_gates(lg):
        for cc in range(N_KEYS // ch):
            gates = [None] * rows_per_tile
            for h in range(PEER_HEADS):
                b = s2_ref[lg, h, cc * ch:(cc + 1) * ch, :]
                eb = e2_ref[lg, h, cc * ch:(cc + 1) * ch, :]
                t = thrb_scr[lg, h]
                for rr in range(rows_per_tile):
                    a = rowb_scr[lg, rr * PEER_HEADS + h]
                    ea = rowb_scr[lg, n_rows + rr * PEER_HEADS + h]
                    term = jnp.where(a + b >= t, ea * eb, 0.0)
                    gates[rr] = term if gates[rr] is None else gates[rr] + term
            for rr in range(rows_per_tile):
                g_scr[lg, rr * N_KEYS + cc * ch:rr * N_KEYS + (cc + 1) * ch, :] = gates[rr]

    for k in range(n_chunks):
        act = _dot(u_ref[...], h2t_ref[k])
        for q in range(lg_per_chunk):
            lg = k * lg_per_chunk + q
            w_scr[:, lg * LANES:(lg + 1) * LANES] = (
                g_scr[lg] * _gelu_sigmoid(act[:, q * LANES:(q + 1) * LANES])).astype(BF16)

    d_slab = acc_scr.shape[0] // lane_groups

    def piece(lg, carry):
        d0 = pl.multiple_of(lg * d_slab, d_slab)
        acc_scr[pl.ds(d0, d_slab), :] += _dot(vt_ref[pl.ds(d0, d_slab), :], w_scr[...])
        build_gates(lg)
        return carry

    lax.fori_loop(0, lane_groups, piece, 0)

    @pl.when(j == n_tiles)
    def _():
        peer = jnp.transpose(acc_scr[...])
        y_ref[...] = x1_ref[...] + gt2_ref[...] * _rms(peer, g3_ref[...])


def _experts(h2t, u_bf, vt_bf, s1, s2, e2, thr, lse, x1, gt2, norm_g, tm, per_row_mod, tiles_per_seq):
    chunk_w = h2t.shape[2]
    n = h2t.shape[0] * chunk_w
    te = 512
    n_tiles = N_EXPERTS // te
    prev_tile = lambda j: jnp.maximum(j - 1, 0)
    if per_row_mod:
        gt2_spec = pl.BlockSpec((tm, D_MODEL), lambda i, j: (i, 0))
    else:
        gt2_spec = pl.BlockSpec((None, 1, D_MODEL), lambda i, j: ((i // tiles_per_seq) * 6 + 5, 0, 0))
    return pl.pallas_call(
        _expert_kernel,
        grid=(n // tm, n_tiles + 1),
        in_specs=[
            pl.BlockSpec((tm // chunk_w, D_MODEL, chunk_w), lambda i, j: (i, 0, 0)),
            pl.BlockSpec((te, D_MODEL), lambda i, j: (prev_tile(j), 0)),
            pl.BlockSpec((D_MODEL, te), lambda i, j: (0, prev_tile(j))),
            pl.BlockSpec((tm // LANES, PEER_HEADS, N_KEYS, LANES), lambda i, j: (i, 0, 0, 0)),
            pl.BlockSpec((tm // LANES, PEER_HEADS, N_KEYS, LANES), lambda i, j: (i, 0, 0, 0)),
            pl.BlockSpec((tm // LANES, PEER_HEADS, N_KEYS, LANES), lambda i, j: (i, 0, 0, 0)),
            pl.BlockSpec((tm // LANES, PEER_HEADS, LANES), lambda i, j: (i, 0, 0)),
            pl.BlockSpec((tm // LANES, PEER_HEADS, LANES), lambda i, j: (i, 0, 0)),
            pl.BlockSpec((tm, D_MODEL), lambda i, j: (i, 0)),
            gt2_spec,
            pl.BlockSpec((None, 1, D_MODEL), lambda i, j: (3, 0, 0)),
        ],
        out_specs=pl.BlockSpec((tm, D_MODEL), lambda i, j: (i, 0)),
        out_shape=jax.ShapeDtypeStruct((n, D_MODEL), F32),
        scratch_shapes=[pltpu.VMEM((D_MODEL, tm), F32), pltpu.VMEM((tm // LANES, te, LANES), F32),
                        pltpu.VMEM((te, tm), BF16),
                        pltpu.VMEM((tm // LANES, 2 * (te // N_KEYS) * PEER_HEADS, 8, LANES), F32),
                        pltpu.VMEM((tm // LANES, PEER_HEADS, 8, LANES), F32),
                        pltpu.VMEM((tm // LANES * (te // N_KEYS), PEER_HEADS, LANES), F32)],
        compiler_params=_cparams(("parallel", "arbitrary")),
        name="peer_experts",
    )(h2t, u_bf, vt_bf, s1, s2, e2, thr, lse, x1, gt2, norm_g)


def _permute_w_in(w_in):
    q = w_in[:, :ATT_WIDTH]
    k = w_in[:, ATT_WIDTH:2 * ATT_WIDTH]
    v = w_in[:, 2 * ATT_WIDTH:3 * ATT_WIDTH]
    parts = [q]
    for g in range(N_DIL):
        parts += [k[:, g * GROUP_W:(g + 1) * GROUP_W], v[:, g * GROUP_W:(g + 1) * GROUP_W]]
    parts.append(w_in[:, 3 * ATT_WIDTH:])
    return jnp.concatenate(parts, axis=1).astype(BF16)


def kernel(x_prompt, x_sample, c_prompt, c_sample, cache_kv_dil1, cache_kv_dil4, cache_kv_dil16, w_ada, b_ada, norm_g, w_in, gm_ln_g, gm_ln_b, w_s, b_s, w_out, w_q, sub_keys, expert_u, expert_v):
    batch, seq, _ = x_prompt.shape
    n_s = x_sample.shape[0] * x_sample.shape[1]
    assert w_ada.shape[0] == 1 and x_sample.shape[1] == 1
    n_p = batch * seq
    pad_s = 128

    w_in_p = _permute_w_in(w_in[0])
    w_out_b = w_out[0].astype(BF16)
    w_q_b = w_q[0].astype(BF16)
    keys_b = sub_keys[0].astype(BF16)
    u_b = expert_u[0].astype(BF16)
    vt_b = jnp.transpose(expert_v[0].astype(BF16))
    ng = norm_g[0].reshape(4, 1, D_MODEL)
    ln_g, ln_b = gm_ln_g[0], gm_ln_b[0]

    rows = batch + n_s
    c_all = jnp.zeros((16, D_MODEL), F32).at[:rows].set(jnp.concatenate([c_prompt, c_sample], axis=0))
    mod = _ada(c_all, w_ada[0], b_ada)
    mod_p = mod.reshape(16 * 6, 1, D_MODEL)
    mod_s = jnp.zeros((pad_s, 6, D_MODEL), F32).at[:n_s].set(mod[batch:rows].reshape(n_s, 6, D_MODEL))
    mods_s = [mod_s[:, k] for k in range(6)]

    xp = x_prompt.reshape(n_p, D_MODEL)
    bsb = jnp.broadcast_to(b_s[0][:, :, None], (GM_HEADS, CHUNK, GM_HD))
    outs = _inproj_prompt(xp, mod_p, ng, w_in_p, w_s[0], bsb, ln_g, ln_b, batch, seq)
    folded, kv_p, gm_p = outs[:9], outs[9:12], outs[12]
    att_p = _attn_prompt(folded, batch, seq)
    tm_p = 256
    x1_p, h2t_p, *sel_p = _outproj_peer(
        xp, att_p, gm_p, [mod_p] * 3, ng, w_out_b, w_q_b, keys_b, tm_p, False, seq // tm_p)
    tm_e = 512
    y_p = _experts(h2t_p, u_b, vt_b, *sel_p, x1_p, mod_p, ng, tm_e, False, seq // tm_e)

    xs = jnp.zeros((pad_s, D_MODEL), F32).at[:n_s].set(x_sample.reshape(n_s, D_MODEL))
    wd = jnp.repeat(w_s[0][:, 0, 0], GM_HD)[None, :]
    b0 = jnp.repeat(b_s[0][:, 0], GM_HD)[None, :]
    q_s, kv0_s, kv1_s, kv2_s, gm_s, gvn_s = _inproj_sample(
        xs, mods_s[0], mods_s[1], ng, w_in_p, wd, b0, ln_g, ln_b)
    new_kv_s = [kv[:n_s].reshape(n_s, 2, ATT_HEADS, ATT_HD) for kv in (kv0_s, kv1_s, kv2_s)]
    caches_t = [jnp.transpose(c[0], (0, 2, 3, 4, 1)) for c in (cache_kv_dil1, cache_kv_dil4, cache_kv_dil16)]
    att_s = _attn_sample(q_s[:n_s].reshape(n_s, N_DIL, ATT_HEADS, ATT_HD, 1),
                         [kv[..., None] for kv in new_kv_s], caches_t)
    att_s = att_s.reshape(n_s, N_DIL, GROUP_W).astype(BF16)
    att_s = [jnp.zeros((pad_s, GROUP_W), BF16).at[:n_s].set(att_s[:, g]) for g in range(N_DIL)]
    x1_s, h2t_s, *sel_s = _outproj_peer(
        xs, att_s, gm_s, mods_s[2:5], ng, w_out_b, w_q_b, keys_b, pad_s, True, 1)
    y_s = _experts(h2t_s, u_b, vt_b, *sel_s, x1_s, mods_s[5], ng, pad_s, True, 1)

    y_prompt = y_p.reshape(batch, seq, D_MODEL)
    y_sample = y_s[:n_s].reshape(x_sample.shape)
    new_kv_p = []
    for kv_t, (win, _) in zip(kv_p, DIL_PAIRS):
        keep = min(win, seq)
        kv_t = kv_t[:, :, seq - keep:].reshape(batch, 2, ATT_HEADS, ATT_HD, keep)
        new_kv_p.append(jnp.transpose(kv_t, (0, 4, 1, 2, 3))[None])
    new_kv_s = [kv.reshape(1, n_s, 1, 2, ATT_HEADS, ATT_HD) for kv in new_kv_s]
    state_gv = gvn_s[:n_s].reshape(1, n_s, 1, GM_HEADS, GM_HD)
    return (y_prompt, y_sample, *new_kv_p, *new_kv_s, state_gv)
```

```python
import functools
import math

import jax
import jax.numpy as jnp
from jax import lax
from jax.experimental import pallas as pl
from jax.experimental.pallas import tpu as pltpu

F32 = jnp.float32
BF16 = jnp.bfloat16

D_MODEL = 2048
ATT_HD = 64
ATT_HEADS = 8
GROUP_W = ATT_HEADS * ATT_HD
DIL_PAIRS = ((128, 1), (512, 4), (2048, 16))
N_DIL = 3
ATT_WIDTH = N_DIL * GROUP_W
QBLK = 128
BAND = 128
CHUNK = 128
GM_HEADS = 4
GM_HD = 128
GM_WIDTH = GM_HEADS * GM_HD
IN_WIDTH = 3 * ATT_WIDTH + 2 * GM_WIDTH
N_KEYS = 128
N_EXPERTS = N_KEYS * N_KEYS
PEER_HEADS = 8
PEER_HALF = 128
PEER_TOPK = 16
EPS = 1e-6
NEG_INF = float("-inf")
LOG2E = 1.4426950408889634

LANES = 128
VMEM_LIMIT = 56 * 1024 * 1024

Q_OFF = 0
KV_OFF = ATT_WIDTH
GU_OFF = ATT_WIDTH + N_DIL * 2 * GROUP_W
GV_OFF = GU_OFF + GM_WIDTH


def _cparams(sem, flags=None):
    return pltpu.CompilerParams(dimension_semantics=sem, vmem_limit_bytes=VMEM_LIMIT, flags=flags)


def _gelu(x):
    c = math.sqrt(2.0 / math.pi)
    return 0.5 * x * (1.0 + jnp.tanh(c * (x + 0.044715 * (x * x * x))))


def _gelu_sigmoid(x):
    k2 = -2.0 * math.sqrt(2.0 / math.pi) * LOG2E
    k1 = k2 * 0.044715
    return x / (1.0 + jnp.exp2(x * (k2 + k1 * (x * x))))


def _rms(x, g):
    return x * lax.rsqrt(jnp.mean(x * x, axis=-1, keepdims=True) + EPS) * g


def _dot(a, b):
    return jnp.dot(a, b, preferred_element_type=F32)


def _dot_nt(a, b):
    return lax.dot_general(a, b, (((1,), (1,)), ((), ())), preferred_element_type=F32)


def _ada_kernel(c_ref, w_ref, b_ref, o_ref):
    c = c_ref[...]
    act = (c / (1.0 + jnp.exp(-c))).astype(BF16)
    o_ref[...] = _dot(act, w_ref[...].astype(BF16)) + b_ref[...]


def _ada(c_all, w_ada, b_ada):
    rows = c_all.shape[0]
    n_out = w_ada.shape[1]
    tn = 1024
    return pl.pallas_call(
        _ada_kernel,
        grid=(n_out // tn,),
        in_specs=[
            pl.BlockSpec((rows, D_MODEL), lambda j: (0, 0)),
            pl.BlockSpec((D_MODEL, tn), lambda j: (0, j)),
            pl.BlockSpec((1, tn), lambda j: (0, j)),
        ],
        out_specs=pl.BlockSpec((rows, tn), lambda j: (0, j)),
        out_shape=jax.ShapeDtypeStruct((rows, n_out), F32),
        compiler_params=_cparams(("parallel",)),
        name="ada_mod",
    )(c_all, w_ada, b_ada)


def _modulated_norm(x_ref, sh_ref, sc_ref, g_ref):
    x = x_ref[...]
    return _rms(x, g_ref[...]) * (1.0 + sc_ref[...]) + sh_ref[...]


def _gm_layernorm(gv, lng_ref, lnb_ref, hh):
    v = gv[:, hh * GM_HD:(hh + 1) * GM_HD]
    mu = jnp.mean(v, axis=-1, keepdims=True)
    var = jnp.mean(jnp.square(v - mu), axis=-1, keepdims=True)
    return (v - mu) * lax.rsqrt(var + EPS) * lng_ref[pl.ds(hh, 1), :] + lnb_ref[pl.ds(hh, 1), :]


def _inproj_prompt_kernel(x_ref, sh_ref, sc_ref, g_ref, w_ref, ws_ref, bsb_ref, lng_ref, lnb_ref,
                          q0_ref, k0_ref, v0_ref, q1_ref, k1_ref, v1_ref, q2_ref, k2_ref, v2_ref,
                          kv0_ref, kv1_ref, kv2_ref, gm_ref, h_scr, y_scr):
    tm = x_ref.shape[0]
    h_scr[...] = _modulated_norm(x_ref, sh_ref, sc_ref, g_ref).astype(BF16)

    def proj(off):
        return _dot(h_scr[...], w_ref[:, off:off + GROUP_W])

    folded = ((q0_ref, k0_ref, v0_ref), (q1_ref, k1_ref, v1_ref), (q2_ref, k2_ref, v2_ref))
    kv_refs = (kv0_ref, kv1_ref, kv2_ref)
    for g, (_, dil) in enumerate(DIL_PAIRS):
        offs = (Q_OFF + g * GROUP_W, KV_OFF + g * 2 * GROUP_W, KV_OFF + g * 2 * GROUP_W + GROUP_W)
        for which, off in enumerate(offs):
            y = proj(off)
            if which > 0:
                kv_refs[g][0, (which - 1) * GROUP_W:which * GROUP_W, :] = jnp.transpose(y)
            else:
                y = y * (ATT_HD ** -0.5)
            dst = folded[g][which]
            if dil == 1:
                dst[...] = y.astype(BF16)
            else:
                for cb in range(GROUP_W // LANES):
                    y_scr[cb] = y[:, cb * LANES:(cb + 1) * LANES]
                for r in range(dil):
                    for cb in range(GROUP_W // LANES):
                        dst[0, r, :, cb * LANES:(cb + 1) * LANES] = (
                            y_scr[cb, pl.ds(r, tm // dil, stride=dil), :].astype(BF16))

    gu = _gelu(proj(GU_OFF))
    gv = _gelu(proj(GV_OFF))
    row = lax.broadcasted_iota(jnp.int32, (CHUNK, CHUNK), 0)
    col = lax.broadcasted_iota(jnp.int32, (CHUNK, CHUNK), 1)
    for hh in range(GM_HEADS):
        gvn = _gm_layernorm(gv, lng_ref, lnb_ref, hh).astype(BF16)
        w_tril = jnp.where(row >= col, ws_ref[hh], 0.0).astype(BF16)
        for ci in range(tm // CHUNK):
            z = _dot(w_tril, gvn[ci * CHUNK:(ci + 1) * CHUNK]) + bsb_ref[hh]
            gu_blk = gu[ci * CHUNK:(ci + 1) * CHUNK, hh * GM_HD:(hh + 1) * GM_HD]
            gm_ref[ci * CHUNK:(ci + 1) * CHUNK, hh * GM_HD:(hh + 1) * GM_HD] = (gu_blk * z).astype(BF16)


def _inproj_prompt(x, mod, norm_g, w_in_p, w_s, bsb, ln_g, ln_b, batch, seq):
    n = x.shape[0]
    tm = 256
    tiles_per_seq = seq // tm

    def mod_spec(k):
        return pl.BlockSpec((None, 1, D_MODEL), lambda i: ((i // tiles_per_seq) * 6 + k, 0, 0))

    def folded_spec(dil):
        if dil == 1:
            return pl.BlockSpec((tm, GROUP_W), lambda i: (i, 0))
        return pl.BlockSpec((1, dil, tm // dil, GROUP_W),
                            lambda i: (i // tiles_per_seq, 0, i % tiles_per_seq, 0))

    def folded_shape(dil):
        if dil == 1:
            return jax.ShapeDtypeStruct((n, GROUP_W), BF16)
        return jax.ShapeDtypeStruct((batch, dil, seq // dil, GROUP_W), BF16)

    out_specs, out_shape = [], []
    for _, dil in DIL_PAIRS:
        for _ in range(3):
            out_specs.append(folded_spec(dil))
            out_shape.append(folded_shape(dil))
    for _ in range(N_DIL):
        out_specs.append(pl.BlockSpec((1, 2 * GROUP_W, tm),
                                      lambda i: (i // tiles_per_seq, 0, i % tiles_per_seq)))
        out_shape.append(jax.ShapeDtypeStruct((batch, 2 * GROUP_W, seq), F32))
    out_specs.append(pl.BlockSpec((tm, GM_WIDTH), lambda i: (i, 0)))
    out_shape.append(jax.ShapeDtypeStruct((n, GM_WIDTH), BF16))

    return pl.pallas_call(
        _inproj_prompt_kernel,
        grid=(n // tm,),
        in_specs=[
            pl.BlockSpec((tm, D_MODEL), lambda i: (i, 0)),
            mod_spec(0), mod_spec(1),
            pl.BlockSpec((None, 1, D_MODEL), lambda i: (0, 0, 0)),
            pl.BlockSpec((D_MODEL, IN_WIDTH), lambda i: (0, 0)),
            pl.BlockSpec((GM_HEADS, CHUNK, CHUNK), lambda i: (0, 0, 0)),
            pl.BlockSpec((GM_HEADS, CHUNK, GM_HD), lambda i: (0, 0, 0)),
            pl.BlockSpec((GM_HEADS, GM_HD), lambda i: (0, 0)),
            pl.BlockSpec((GM_HEADS, GM_HD), lambda i: (0, 0)),
        ],
        out_specs=out_specs,
        out_shape=out_shape,
        scratch_shapes=[pltpu.VMEM((tm, D_MODEL), BF16), pltpu.VMEM((GROUP_W // LANES, tm, LANES), F32)],
        compiler_params=_cparams(("parallel",)),
        name="inproj_prompt",
    )(x, mod, mod, norm_g, w_in_p, w_s, bsb, ln_g, ln_b)


def _inproj_sample_kernel(x_ref, sh_ref, sc_ref, g_ref, w_ref, wd_ref, b0_ref, lng_ref, lnb_ref,
                          q_ref, kv0_ref, kv1_ref, kv2_ref, gm_ref, gvn_ref):
    h = _modulated_norm(x_ref, sh_ref, sc_ref, g_ref).astype(BF16)
    q_ref[...] = _dot(h, w_ref[:, Q_OFF:Q_OFF + ATT_WIDTH])
    for g, kv_ref in enumerate((kv0_ref, kv1_ref, kv2_ref)):
        off = KV_OFF + g * 2 * GROUP_W
        kv_ref[...] = _dot(h, w_ref[:, off:off + 2 * GROUP_W])
    gu = _gelu(_dot(h, w_ref[:, GU_OFF:GU_OFF + GM_WIDTH]))
    gv = _gelu(_dot(h, w_ref[:, GV_OFF:GV_OFF + GM_WIDTH]))
    for hh in range(GM_HEADS):
        sl = slice(hh * GM_HD, (hh + 1) * GM_HD)
        gvn = _gm_layernorm(gv, lng_ref, lnb_ref, hh)
        gvn_ref[:, sl] = gvn
        z = wd_ref[:, sl] * gvn + b0_ref[:, sl]
        gm_ref[:, sl] = (gu[:, sl] * z).astype(BF16)


def _inproj_sample(x, sh, sc, norm_g, w_in_p, wd, b0, ln_g, ln_b):
    n = x.shape[0]
    full = lambda shape: pl.BlockSpec(shape, lambda i: tuple(0 for _ in shape))
    return pl.pallas_call(
        _inproj_sample_kernel,
        grid=(1,),
        in_specs=[
            full((n, D_MODEL)), full((n, D_MODEL)), full((n, D_MODEL)),
            pl.BlockSpec((None, 1, D_MODEL), lambda i: (0, 0, 0)),
            full((D_MODEL, IN_WIDTH)),
            full((1, GM_WIDTH)), full((1, GM_WIDTH)),
            full((GM_HEADS, GM_HD)), full((GM_HEADS, GM_HD)),
        ],
        out_specs=[full((n, ATT_WIDTH)), full((n, 2 * GROUP_W)), full((n, 2 * GROUP_W)),
                   full((n, 2 * GROUP_W)), full((n, GM_WIDTH)), full((n, GM_WIDTH))],
        out_shape=[jax.ShapeDtypeStruct((n, ATT_WIDTH), F32)]
        + [jax.ShapeDtypeStruct((n, 2 * GROUP_W), F32)] * 3
        + [jax.ShapeDtypeStruct((n, GM_WIDTH), BF16), jax.ShapeDtypeStruct((n, GM_WIDTH), F32)],
        compiler_params=_cparams(("arbitrary",)),
        name="inproj_sample",
    )(x, sh, sc, norm_g, w_in_p, wd, b0, ln_g, ln_b)


def _attn_block(qt, kt, vt, off, nkeys):
    lane = lax.broadcasted_iota(jnp.int32, (1, LANES), 1)
    lo = lane < ATT_HD
    qi = lax.broadcasted_iota(jnp.int32, (QBLK, nkeys), 0)
    kk = lax.broadcasted_iota(jnp.int32, (QBLK, nkeys), 1)
    dist = qi - kk + off
    valid = lax.bitcast_convert_type(dist, jnp.uint32) <= jnp.uint32(BAND)
    res = []
    for sel in (lo, jnp.logical_not(lo)):
        qm = jnp.where(sel, qt, jnp.zeros_like(qt))
        s = jnp.where(valid, _dot_nt(qm, kt), NEG_INF)
        m = jnp.max(s, axis=-1, keepdims=True)
        p = jnp.exp(s - m)
        den = jnp.sum(p, axis=-1, keepdims=True)
        o = _dot(p.astype(BF16), vt) * (1.0 / den)
        res.append((o, m + jnp.log(den)))
    o = jnp.where(lo, res[0][0], res[1][0])
    lse = jnp.where(lo, res[0][1], res[1][1])
    return o, lse


def _attn_prompt_kernel(q0_ref, k0_ref, v0_ref, q1_ref, k1_ref, v1_ref, q2_ref, k2_ref, v2_ref,
                        a0_ref, a1_ref, a2_ref, of_scr, lf_scr, to_scr, tl_scr):
    seq = q0_ref.shape[0]

    def run_block(g, qt, kt, vt, off, nkeys, row0):
        o, lse = _attn_block(qt, kt, vt, off, nkeys)
        of_scr[g, pl.ds(row0, QBLK), :] = o
        lf_scr[g, pl.ds(row0, QBLK), :] = lse

    par = 8

    def g0_body(it, carry):
        for u in range(par):
            qb = it * par + u
            qs = pl.multiple_of(qb * QBLK, QBLK)
            ks = pl.multiple_of(jnp.maximum(qb - 1, 0) * QBLK, QBLK)
            run_block(0, q0_ref[pl.ds(qs, QBLK), :], k0_ref[pl.ds(ks, 2 * QBLK), :],
                      v0_ref[pl.ds(ks, 2 * QBLK), :], qs - ks, 2 * QBLK, qs)
        return carry

    lax.fori_loop(0, seq // QBLK // par, g0_body, 0)

    dil1 = DIL_PAIRS[1][1]
    len1 = seq // dil1

    res_per_body = max(par // (len1 // QBLK), 1)

    def g1_body(it, carry):
        for u in range(res_per_body):
            r = it * res_per_body + u
            for qb in range(len1 // QBLK):
                qs = qb * QBLK
                ks = max(qb - 1, 0) * QBLK
                run_block(1, q1_ref[0, r, pl.ds(qs, QBLK), :], k1_ref[0, r, pl.ds(ks, 2 * QBLK), :],
                          v1_ref[0, r, pl.ds(ks, 2 * QBLK), :], qs - ks, 2 * QBLK,
                          pl.multiple_of(r * len1 + qs, QBLK))
        return carry

    lax.fori_loop(0, dil1 // res_per_body, g1_body, 0)

    dil2 = DIL_PAIRS[2][1]
    len2 = seq // dil2

    def g2_body(it, carry):
        for u in range(par):
            r = it * par + u
            pad = it * par + (u ^ 1)
            kt = jnp.concatenate([k2_ref[0, r], k2_ref[0, pad]], axis=0)
            vt = jnp.concatenate([v2_ref[0, r], v2_ref[0, pad]], axis=0)
            run_block(2, q2_ref[0, r], kt, vt, 0, 2 * QBLK, pl.multiple_of(r * len2, QBLK))
        return carry

    lax.fori_loop(0, dil2 // par, g2_body, 0)

    def mix_body(nt, carry):
        t0 = pl.multiple_of(nt * QBLK, QBLK)
        o_nat = [of_scr[0, pl.ds(t0, QBLK), :]]
        l_nat = [lf_scr[0, pl.ds(t0, QBLK), :]]
        for g, dil, length in ((1, dil1, len1), (2, dil2, len2)):
            per = QBLK // dil
            for r in range(dil):
                src = pl.multiple_of(r * length + nt * per, per)
                to_scr[g - 1, pl.ds(r, per, stride=dil), :] = of_scr[g, pl.ds(src, per), :]
                tl_scr[g - 1, pl.ds(r, per, stride=dil), :] = lf_scr[g, pl.ds(src, per), :]
            o_nat.append(to_scr[g - 1])
            l_nat.append(tl_scr[g - 1])
        mx = jnp.maximum(jnp.maximum(l_nat[0], l_nat[1]), l_nat[2])
        e = [jnp.exp(l - mx) for l in l_nat]
        inv = 1.0 / (e[0] + e[1] + e[2])
        for g, a_ref in enumerate((a0_ref, a1_ref, a2_ref)):
            a_ref[pl.ds(t0, QBLK), :] = (o_nat[g] * (e[g] * inv)).astype(BF16)
        return carry

    lax.fori_loop(0, seq // QBLK, mix_body, 0)


def _attn_prompt(folded, batch, seq):
    n = batch * seq
    pairs = GROUP_W // LANES
    in_specs = []
    for _, dil in DIL_PAIRS:
        for _ in range(3):
            if dil == 1:
                in_specs.append(pl.BlockSpec((seq, LANES), lambda b, hp: (b, hp)))
            else:
                in_specs.append(pl.BlockSpec((1, dil, seq // dil, LANES), lambda b, hp: (b, 0, 0, hp)))
    return pl.pallas_call(
        _attn_prompt_kernel,
        grid=(batch, pairs),
        in_specs=in_specs,
        out_specs=[pl.BlockSpec((seq, LANES), lambda b, hp: (b, hp))] * N_DIL,
        out_shape=[jax.ShapeDtypeStruct((n, GROUP_W), BF16)] * N_DIL,
        scratch_shapes=[pltpu.VMEM((N_DIL, seq, LANES), F32), pltpu.VMEM((N_DIL, seq, LANES), F32),
                        pltpu.VMEM((N_DIL - 1, QBLK, LANES), F32), pltpu.VMEM((N_DIL - 1, QBLK, LANES), F32)],
        compiler_params=_cparams(("parallel", "parallel")),
        name="attn_prompt",
    )(*folded)


def _attn_sample_kernel(q_ref, n0_ref, n1_ref, n2_ref, c0_ref, c1_ref, c2_ref, o_ref):
    groups = ((n0_ref, c0_ref), (n1_ref, c1_ref), (n2_ref, c2_ref))

    def head(h, carry):
        outs, lses = [], []
        for g, (n_ref, c_ref) in enumerate(groups):
            dil = DIL_PAIRS[g][1]
            nbuf = c_ref.shape[-1]
            q = q_ref[0, g, h] * (ATT_HD ** -0.5)
            row = lax.broadcasted_iota(jnp.int32, (1, nbuf), 1)
            valid = jnp.bitwise_and(row, dil - 1) == 0
            s_buf = jnp.sum(c_ref[0, 0, h] * q, axis=0, keepdims=True)
            s_buf = jnp.where(valid, s_buf, NEG_INF)
            s_new = jnp.sum(n_ref[0, 0, h] * q, axis=0, keepdims=True)
            m = jnp.maximum(jnp.max(s_buf, axis=1, keepdims=True), s_new)
            p_buf = jnp.exp(s_buf - m)
            p_new = jnp.exp(s_new - m)
            den = jnp.sum(p_buf, axis=1, keepdims=True) + p_new
            o = jnp.sum(c_ref[0, 1, h] * p_buf, axis=1, keepdims=True) + n_ref[0, 1, h] * p_new
            outs.append(o / den)
            lses.append(m + jnp.log(den))
        mx = jnp.maximum(jnp.maximum(lses[0], lses[1]), lses[2])
        e = [jnp.exp(l - mx) for l in lses]
        inv = 1.0 / (e[0] + e[1] + e[2])
        for g in range(N_DIL):
            o_ref[0, g, h] = outs[g] * (e[g] * inv)
        return carry

    lax.fori_loop(0, ATT_HEADS, head, 0)


def _attn_sample(q, new_kvs, caches):
    n_real = q.shape[0]
    assert all(dil & (dil - 1) == 0 and c.shape[-1] == BAND * dil for c, (_, dil) in zip(caches, DIL_PAIRS))
    per_row = lambda a: pl.BlockSpec((1,) + a.shape[1:], lambda b: (b,) + (0,) * (a.ndim - 1))
    return pl.pallas_call(
        _attn_sample_kernel,
        grid=(n_real,),
        in_specs=[per_row(a) for a in (q, *new_kvs, *caches)],
        out_specs=per_row(q),
        out_shape=jax.ShapeDtypeStruct(q.shape, F32),
        compiler_params=_cparams(("arbitrary",)),
        name="attn_sample",
    )(q, *new_kvs, *caches)


SUBLANES = 8


def _oddeven_merge_sort_pairs(n):
    pairs = []
    p = 1
    while p < n:
        k = p
        while k >= 1:
            for j in range(k % p, n - k, 2 * k):
                for i in range(min(k, n - j - k)):
                    if (i + j) // (2 * p) == (i + j + k) // (2 * p):
                        pairs.append((i + j, i + j + k))
            k //= 2
        p *= 2
    return pairs


def _topk_desc(sc, k):
    n = sc.shape[0] // SUBLANES
    rows = [sc[SUBLANES * i:SUBLANES * (i + 1)] for i in range(n)]
    for i, j in _oddeven_merge_sort_pairs(n):
        rows[i], rows[j] = jnp.maximum(rows[i], rows[j]), jnp.minimum(rows[i], rows[j])
    vals = []
    for it in range(k):
        m = jnp.max(rows[0], axis=0, keepdims=True)
        vals.append(m)
        if it + 1 < k:
            win = rows[0] == m
            depth = min(n, k - it)
            for r in range(depth - 1):
                rows[r] = jnp.where(win, rows[r + 1], rows[r])
            rows[depth - 1] = jnp.where(win, NEG_INF, rows[depth - 1])
    return vals


def _outproj_peer_kernel(x_ref, a0_ref, a1_ref, a2_ref, gm_ref, gt1_ref, sh2_ref, sc2_ref, g1_ref, g2_ref,
                         wo_ref, wq_ref, keys_ref,
                         x1_ref, h2t_ref, s1_ref, s2_ref, e2_ref, thr_ref, lse_ref):
    mo = _dot(a0_ref[...], wo_ref[0:GROUP_W, :])
    mo += _dot(a1_ref[...], wo_ref[GROUP_W:2 * GROUP_W, :])
    mo += _dot(a2_ref[...], wo_ref[2 * GROUP_W:3 * GROUP_W, :])
    mo += _dot(gm_ref[...], wo_ref[3 * GROUP_W:4 * GROUP_W, :])
    x1 = x_ref[...] + gt1_ref[...] * _rms(mo, g1_ref[...])
    x1_ref[...] = x1
    h2 = _rms(x1, g2_ref[...]) * (1.0 + sc2_ref[...]) + sh2_ref[...]
    h2t_ref[0] = jnp.transpose(h2).astype(BF16)
    qr = _dot(h2.astype(BF16), wq_ref[...])
    lane_groups = thr_ref.shape[0]
    for h in range(PEER_HEADS):
        tops = []
        for p, s_ref in enumerate((s1_ref, s2_ref)):
            c0 = (h * 2 + p) * PEER_HALF
            sc = _dot_nt(keys_ref[p], qr[:, c0:c0 + PEER_HALF].astype(BF16)) * LOG2E
            for lg in range(lane_groups):
                s_ref[lg, h] = sc[:, lg * LANES:(lg + 1) * LANES]
            tops.append(_topk_desc(sc, PEER_TOPK))
            if p == 1:
                e2 = jnp.exp2(sc - tops[1][0])
                for lg in range(lane_groups):
                    e2_ref[lg, h] = e2[:, lg * LANES:(lg + 1) * LANES]
        t1 = jnp.concatenate(tops[0], axis=0)
        half = PEER_TOPK // 2
        cand = jnp.concatenate(
            [t1 + tops[1][0]]
            + [t1[:half] + tops[1][b] for b in range(1, half)]
            + [jnp.concatenate(tops[1][half:], axis=0) + tops[0][0]], axis=0)
        best = _topk_desc(cand, PEER_TOPK)
        z = jnp.ones_like(best[0])
        for v in best[1:]:
            z = z + jnp.exp2(v - best[0])
        lse = best[0] + jnp.log(z) * LOG2E - tops[1][0]
        for lg in range(lane_groups):
            thr_ref[lg, pl.ds(h, 1), :] = best[-1][:, lg * LANES:(lg + 1) * LANES]
            lse_ref[lg, pl.ds(h, 1), :] = lse[:, lg * LANES:(lg + 1) * LANES]


def _outproj_peer(x, atts, gm, mods, norm_g, w_out, w_q, keys, tm, per_row_mod, tiles_per_seq):
    n = x.shape[0]
    if per_row_mod:
        mod_specs = [pl.BlockSpec((tm, D_MODEL), lambda i: (i, 0))] * 3
    else:
        mod_specs = [pl.BlockSpec((None, 1, D_MODEL), lambda i, k=k: ((i // tiles_per_seq) * 6 + k, 0, 0))
                     for k in (2, 3, 4)]
    row_spec = lambda w: pl.BlockSpec((tm, w), lambda i: (i, 0))
    const = lambda shape: pl.BlockSpec(shape, lambda i: tuple(0 for _ in shape))
    return pl.pallas_call(
        _outproj_peer_kernel,
        grid=(n // tm,),
        in_specs=[row_spec(D_MODEL)] + [row_spec(GROUP_W)] * 4 + mod_specs
        + [pl.BlockSpec((None, 1, D_MODEL), lambda i: (1, 0, 0)),
           pl.BlockSpec((None, 1, D_MODEL), lambda i: (2, 0, 0)),
           const((D_MODEL, D_MODEL)), const((D_MODEL, D_MODEL)),
           const((2, N_KEYS, PEER_HALF))],
        out_specs=[row_spec(D_MODEL),
                   pl.BlockSpec((1, D_MODEL, tm), lambda i: (i, 0, 0)),
                   pl.BlockSpec((tm // LANES, PEER_HEADS, N_KEYS, LANES), lambda i: (i, 0, 0, 0)),
                   pl.BlockSpec((tm // LANES, PEER_HEADS, N_KEYS, LANES), lambda i: (i, 0, 0, 0)),
                   pl.BlockSpec((tm // LANES, PEER_HEADS, N_KEYS, LANES), lambda i: (i, 0, 0, 0)),
                   pl.BlockSpec((tm // LANES, PEER_HEADS, LANES), lambda i: (i, 0, 0)),
                   pl.BlockSpec((tm // LANES, PEER_HEADS, LANES), lambda i: (i, 0, 0))],
        out_shape=[jax.ShapeDtypeStruct((n, D_MODEL), F32),
                   jax.ShapeDtypeStruct((n // tm, D_MODEL, tm), BF16),
                   jax.ShapeDtypeStruct((n // LANES, PEER_HEADS, N_KEYS, LANES), F32),
                   jax.ShapeDtypeStruct((n // LANES, PEER_HEADS, N_KEYS, LANES), F32),
                   jax.ShapeDtypeStruct((n // LANES, PEER_HEADS, N_KEYS, LANES), F32),
                   jax.ShapeDtypeStruct((n // LANES, PEER_HEADS, LANES), F32),
                   jax.ShapeDtypeStruct((n // LANES, PEER_HEADS, LANES), F32)],
        compiler_params=_cparams(("parallel",)),
        name="outproj_peer",
    )(x, *atts, gm, *mods, norm_g, norm_g, w_out, w_q, keys)


def _expert_kernel(h2t_ref, u_ref, vt_ref, s1_ref, s2_ref, e2_ref, thr_ref, lse_ref, x1_ref, gt2_ref, g3_ref,
                   y_ref, acc_scr, g_scr, w_scr, rowb_scr, thrb_scr, tile_scr):
    j = pl.program_id(1)
    n_tiles = pl.num_programs(1) - 1
    te = u_ref.shape[0]
    rows_per_tile = te // N_KEYS
    lane_groups = s1_ref.shape[0]
    n_chunks, _, chunk_w = h2t_ref.shape
    lg_per_chunk = chunk_w // LANES
    n_rows = rows_per_tile * PEER_HEADS
    ch = 8

    @pl.when(j == 0)
    def _():
        acc_scr[...] = jnp.zeros_like(acc_scr)
        g_scr[...] = jnp.zeros_like(g_scr)
        for lg in range(lane_groups):
            for h in range(PEER_HEADS):
                thrb_scr[lg, h] = jnp.broadcast_to(thr_ref[lg, pl.ds(h, 1), :], (ch, LANES))

    jt = jnp.minimum(j, n_tiles - 1)
    for lg in range(lane_groups):
        for rr in range(rows_per_tile):
            tile = tile_scr.at[lg * rows_per_tile + rr]
            for h in range(PEER_HEADS):
                tile[pl.ds(h, 1), :] = s1_ref[lg, h, pl.ds(jt * rows_per_tile + rr, 1), :]
            rows8 = tile[...]
            e8 = jnp.exp2(rows8 - lse_ref[lg])
            for h in range(PEER_HEADS):
                rowb_scr[lg, rr * PEER_HEADS + h] = jnp.broadcast_to(rows8[h:h + 1], (ch, LANES))
                rowb_scr[lg, n_rows + rr * PEER_HEADS + h] = jnp.broadcast_to(e8[h:h + 1], (ch, LANES))

    def build_gates(lg):
        for cc in range(N_KEYS // ch):
            gates = [None] * rows_per_tile
            for h in range(PEER_HEADS):
                b = s2_ref[lg, h, cc * ch:(cc + 1) * ch, :]
                eb = e2_ref[lg, h, cc * ch:(cc + 1) * ch, :]
                t = thrb_scr[lg, h]
                for rr in range(rows_per_tile):
                    a = rowb_scr[lg, rr * PEER_HEADS + h]
                    ea = rowb_scr[lg, n_rows + rr * PEER_HEADS + h]
                    term = jnp.where(a + b >= t, ea * eb, 0.0)
                    gates[rr] = term if gates[rr] is None else gates[rr] + term
            for rr in range(rows_per_tile):
                g_scr[lg, rr * N_KEYS + cc * ch:rr * N_KEYS + (cc + 1) * ch, :] = gates[rr]

    for k in range(n_chunks):
        act = _dot(u_ref[...], h2t_ref[k])
        for q in range(lg_per_chunk):
            lg = k * lg_per_chunk + q
            w_scr[:, lg * LANES:(lg + 1) * LANES] = (
                g_scr[lg] * _gelu_sigmoid(act[:, q * LANES:(q + 1) * LANES])).astype(BF16)

    d_slab = acc_scr.shape[0] // lane_groups

    def piece(lg, carry):
        d0 = pl.multiple_of(lg * d_slab, d_slab)
        acc_scr[pl.ds(d0, d_slab), :] += _dot(vt_ref[pl.ds(d0, d_slab), :], w_scr[...])
        build_gates(lg)
        return carry

    lax.fori_loop(0, lane_groups, piece, 0)

    @pl.when(j == n_tiles)
    def _():
        peer = jnp.transpose(acc_scr[...])
        y_ref[...] = x1_ref[...] + gt2_ref[...] * _rms(peer, g3_ref[...])


def _experts(h2t, u_bf, vt_bf, s1, s2, e2, thr, lse, x1, gt2, norm_g, tm, per_row_mod, tiles_per_seq):
    chunk_w = h2t.shape[2]
    n = h2t.shape[0] * chunk_w
    te = 512
    n_tiles = N_EXPERTS // te
    prev_tile = lambda j: jnp.maximum(j - 1, 0)
    if per_row_mod:
        gt2_spec = pl.BlockSpec((tm, D_MODEL), lambda i, j: (i, 0))
    else:
        gt2_spec = pl.BlockSpec((None, 1, D_MODEL), lambda i, j: ((i // tiles_per_seq) * 6 + 5, 0, 0))
    return pl.pallas_call(
        _expert_kernel,
        grid=(n // tm, n_tiles + 1),
        in_specs=[
            pl.BlockSpec((tm // chunk_w, D_MODEL, chunk_w), lambda i, j: (i, 0, 0)),
            pl.BlockSpec((te, D_MODEL), lambda i, j: (prev_tile(j), 0)),
            pl.BlockSpec((D_MODEL, te), lambda i, j: (0, prev_tile(j))),
            pl.BlockSpec((tm // LANES, PEER_HEADS, N_KEYS, LANES), lambda i, j: (i, 0, 0, 0)),
            pl.BlockSpec((tm // LANES, PEER_HEADS, N_KEYS, LANES), lambda i, j: (i, 0, 0, 0)),
            pl.BlockSpec((tm // LANES, PEER_HEADS, N_KEYS, LANES), lambda i, j: (i, 0, 0, 0)),
            pl.BlockSpec((tm // LANES, PEER_HEADS, LANES), lambda i, j: (i, 0, 0)),
            pl.BlockSpec((tm // LANES, PEER_HEADS, LANES), lambda i, j: (i, 0, 0)),
            pl.BlockSpec((tm, D_MODEL), lambda i, j: (i, 0)),
            gt2_spec,
            pl.BlockSpec((None, 1, D_MODEL), lambda i, j: (3, 0, 0)),
        ],
        out_specs=pl.BlockSpec((tm, D_MODEL), lambda i, j: (i, 0)),
        out_shape=jax.ShapeDtypeStruct((n, D_MODEL), F32),
        scratch_shapes=[pltpu.VMEM((D_MODEL, tm), F32), pltpu.VMEM((tm // LANES, te, LANES), F32),
                        pltpu.VMEM((te, tm), BF16),
                        pltpu.VMEM((tm // LANES, 2 * (te // N_KEYS) * PEER_HEADS, 8, LANES), F32),
                        pltpu.VMEM((tm // LANES, PEER_HEADS, 8, LANES), F32),
                        pltpu.VMEM((tm // LANES * (te // N_KEYS), PEER_HEADS, LANES), F32)],
        compiler_params=_cparams(("parallel", "arbitrary")),
        name="peer_experts",
    )(h2t, u_bf, vt_bf, s1, s2, e2, thr, lse, x1, gt2, norm_g)


def _cast_transpose_kernel(v_ref, o_ref):
    o_ref[...] = jnp.transpose(v_ref[...]).astype(BF16)


def _cast_transpose(v):
    rows, cols = v.shape
    tr = 1024
    return pl.pallas_call(
        _cast_transpose_kernel,
        grid=(rows // tr,),
        in_specs=[pl.BlockSpec((tr, cols), lambda i: (i, 0))],
        out_specs=pl.BlockSpec((cols, tr), lambda i: (0, i)),
        out_shape=jax.ShapeDtypeStruct((cols, rows), BF16),
        compiler_params=_cparams(("parallel",)),
        name="cast_transpose",
    )(v)


def _permute_w_in(w_in):
    q = w_in[:, :ATT_WIDTH]
    k = w_in[:, ATT_WIDTH:2 * ATT_WIDTH]
    v = w_in[:, 2 * ATT_WIDTH:3 * ATT_WIDTH]
    parts = [q]
    for g in range(N_DIL):
        parts += [k[:, g * GROUP_W:(g + 1) * GROUP_W], v[:, g * GROUP_W:(g + 1) * GROUP_W]]
    parts.append(w_in[:, 3 * ATT_WIDTH:])
    return jnp.concatenate(parts, axis=1).astype(BF16)


def kernel(x_prompt, x_sample, c_prompt, c_sample, cache_kv_dil1, cache_kv_dil4, cache_kv_dil16, w_ada, b_ada, norm_g, w_in, gm_ln_g, gm_ln_b, w_s, b_s, w_out, w_q, sub_keys, expert_u, expert_v):
    batch, seq, _ = x_prompt.shape
    n_s = x_sample.shape[0] * x_sample.shape[1]
    assert w_ada.shape[0] == 1 and x_sample.shape[1] == 1
    n_p = batch * seq
    pad_s = 128

    w_in_p = _permute_w_in(w_in[0])
    w_out_b = w_out[0].astype(BF16)
    w_q_b = w_q[0].astype(BF16)
    keys_b = sub_keys[0].astype(BF16)
    u_b = expert_u[0].astype(BF16)
    vt_b = _cast_transpose(expert_v[0])
    ng = norm_g[0].reshape(4, 1, D_MODEL)
    ln_g, ln_b = gm_ln_g[0], gm_ln_b[0]

    rows = batch + n_s
    c_all = jnp.zeros((16, D_MODEL), F32).at[:rows].set(jnp.concatenate([c_prompt, c_sample], axis=0))
    mod = _ada(c_all, w_ada[0], b_ada)
    mod_p = mod.reshape(16 * 6, 1, D_MODEL)
    mod_s = jnp.zeros((pad_s, 6, D_MODEL), F32).at[:n_s].set(mod[batch:rows].reshape(n_s, 6, D_MODEL))
    mods_s = [mod_s[:, k] for k in range(6)]

    xp = x_prompt.reshape(n_p, D_MODEL)
    bsb = jnp.broadcast_to(b_s[0][:, :, None], (GM_HEADS, CHUNK, GM_HD))
    outs = _inproj_prompt(xp, mod_p, ng, w_in_p, w_s[0], bsb, ln_g, ln_b, batch, seq)
    folded, kv_p, gm_p = outs[:9], outs[9:12], outs[12]
    att_p = _attn_prompt(folded, batch, seq)
    tm_p = 256
    x1_p, h2t_p, *sel_p = _outproj_peer(
        xp, att_p, gm_p, [mod_p] * 3, ng, w_out_b, w_q_b, keys_b, tm_p, False, seq // tm_p)
    tm_e = 512
    y_p = _experts(h2t_p, u_b, vt_b, *sel_p, x1_p, mod_p, ng, tm_e, False, seq // tm_e)

    xs = jnp.zeros((pad_s, D_MODEL), F32).at[:n_s].set(x_sample.reshape(n_s, D_MODEL))
    wd = jnp.repeat(w_s[0][:, 0, 0], GM_HD)[None, :]
    b0 = jnp.repeat(b_s[0][:, 0], GM_HD)[None, :]
    q_s, kv0_s, kv1_s, kv2_s, gm_s, gvn_s = _inproj_sample(
        xs, mods_s[0], mods_s[1], ng, w_in_p, wd, b0, ln_g, ln_b)
    new_kv_s = [kv[:n_s].reshape(n_s, 2, ATT_HEADS, ATT_HD) for kv in (kv0_s, kv1_s, kv2_s)]
    caches_t = [jnp.transpose(c[0], (0, 2, 3, 4, 1)) for c in (cache_kv_dil1, cache_kv_dil4, cache_kv_dil16)]
    att_s = _attn_sample(q_s[:n_s].reshape(n_s, N_DIL, ATT_HEADS, ATT_HD, 1),
                         [kv[..., None] for kv in new_kv_s], caches_t)
    att_s = att_s.reshape(n_s, N_DIL, GROUP_W).astype(BF16)
    att_s = [jnp.zeros((pad_s, GROUP_W), BF16).at[:n_s].set(att_s[:, g]) for g in range(N_DIL)]
    x1_s, h2t_s, *sel_s = _outproj_peer(
        xs, att_s, gm_s, mods_s[2:5], ng, w_out_b, w_q_b, keys_b, pad_s, True, 1)
    y_s = _experts(h2t_s, u_b, vt_b, *sel_s, x1_s, mods_s[5], ng, pad_s, True, 1)

    y_prompt = y_p.reshape(batch, seq, D_MODEL)
    y_sample = y_s[:n_s].reshape(x_sample.shape)
    new_kv_p = []
    for kv_t, (win, _) in zip(kv_p, DIL_PAIRS):
        keep = min(win, seq)
        kv_t = kv_t[:, :, seq - keep:].reshape(batch, 2, ATT_HEADS, ATT_HD, keep)
        new_kv_p.append(jnp.transpose(kv_t, (0, 4, 1, 2, 3))[None])
    new_kv_s = [kv.reshape(1, n_s, 1, 2, ATT_HEADS, ATT_HD) for kv in new_kv_s]
    state_gv = gvn_s[:n_s].reshape(1, n_s, 1, GM_HEADS, GM_HD)
    return (y_prompt, y_sample, *new_kv_p, *new_kv_s, state_gv)
```

```python
import functools
import math

import jax
import jax.numpy as jnp
from jax import lax
from jax.experimental import pallas as pl
from jax.experimental.pallas import tpu as pltpu

F32 = jnp.float32
BF16 = jnp.bfloat16

D_MODEL = 2048
ATT_HD = 64
ATT_HEADS = 8
GROUP_W = ATT_HEADS * ATT_HD
DIL_PAIRS = ((128, 1), (512, 4), (2048, 16))
N_DIL = 3
ATT_WIDTH = N_DIL * GROUP_W
QBLK = 128
BAND = 128
CHUNK = 128
GM_HEADS = 4
GM_HD = 128
GM_WIDTH = GM_HEADS * GM_HD
IN_WIDTH = 3 * ATT_WIDTH + 2 * GM_WIDTH
N_KEYS = 128
N_EXPERTS = N_KEYS * N_KEYS
PEER_HEADS = 8
PEER_HALF = 128
PEER_TOPK = 16
EPS = 1e-6
NEG_INF = float("-inf")
LOG2E = 1.4426950408889634

LANES = 128
VMEM_LIMIT = 56 * 1024 * 1024

Q_OFF = 0
KV_OFF = ATT_WIDTH
GU_OFF = ATT_WIDTH + N_DIL * 2 * GROUP_W
GV_OFF = GU_OFF + GM_WIDTH


def _cparams(sem, flags=None):
    return pltpu.CompilerParams(dimension_semantics=sem, vmem_limit_bytes=VMEM_LIMIT, flags=flags)


def _gelu(x):
    c = math.sqrt(2.0 / math.pi)
    return 0.5 * x * (1.0 + jnp.tanh(c * (x + 0.044715 * (x * x * x))))


def _gelu_sigmoid(x):
    k2 = -2.0 * math.sqrt(2.0 / math.pi) * LOG2E
    k1 = k2 * 0.044715
    return x / (1.0 + jnp.exp2(x * (k2 + k1 * (x * x))))


def _rms(x, g):
    return x * lax.rsqrt(jnp.mean(x * x, axis=-1, keepdims=True) + EPS) * g


def _dot(a, b):
    return jnp.dot(a, b, preferred_element_type=F32)


def _dot_nt(a, b):
    return lax.dot_general(a, b, (((1,), (1,)), ((), ())), preferred_element_type=F32)


def _ada_kernel(c_ref, w_ref, b_ref, o_ref):
    c = c_ref[...]
    act = (c / (1.0 + jnp.exp(-c))).astype(BF16)
    o_ref[...] = _dot(act, w_ref[...].astype(BF16)) + b_ref[...]


def _ada(c_all, w_ada, b_ada):
    rows = c_all.shape[0]
    n_out = w_ada.shape[1]
    tn = 1024
    return pl.pallas_call(
        _ada_kernel,
        grid=(n_out // tn,),
        in_specs=[
            pl.BlockSpec((rows, D_MODEL), lambda j: (0, 0)),
            pl.BlockSpec((D_MODEL, tn), lambda j: (0, j)),
            pl.BlockSpec((1, tn), lambda j: (0, j)),
        ],
        out_specs=pl.BlockSpec((rows, tn), lambda j: (0, j)),
        out_shape=jax.ShapeDtypeStruct((rows, n_out), F32),
        compiler_params=_cparams(("parallel",)),
        name="ada_mod",
    )(c_all, w_ada, b_ada)


def _modulated_norm(x_ref, sh_ref, sc_ref, g_ref):
    x = x_ref[...]
    return _rms(x, g_ref[...]) * (1.0 + sc_ref[...]) + sh_ref[...]


def _gm_layernorm(gv, lng_ref, lnb_ref, hh):
    v = gv[:, hh * GM_HD:(hh + 1) * GM_HD]
    mu = jnp.mean(v, axis=-1, keepdims=True)
    var = jnp.mean(jnp.square(v - mu), axis=-1, keepdims=True)
    return (v - mu) * lax.rsqrt(var + EPS) * lng_ref[pl.ds(hh, 1), :] + lnb_ref[pl.ds(hh, 1), :]


def _inproj_prompt_kernel(x_ref, sh_ref, sc_ref, g_ref, w_ref, ws_ref, bsb_ref, lng_ref, lnb_ref,
                          q0_ref, k0_ref, v0_ref, q1_ref, k1_ref, v1_ref, q2_ref, k2_ref, v2_ref,
                          kv0_ref, kv1_ref, kv2_ref, gm_ref, h_scr, y_scr):
    tm = x_ref.shape[0]
    h_scr[...] = _modulated_norm(x_ref, sh_ref, sc_ref, g_ref).astype(BF16)

    def proj(off):
        return _dot(h_scr[...], w_ref[:, off:off + GROUP_W])

    folded = ((q0_ref, k0_ref, v0_ref), (q1_ref, k1_ref, v1_ref), (q2_ref, k2_ref, v2_ref))
    kv_refs = (kv0_ref, kv1_ref, kv2_ref)
    for g, (_, dil) in enumerate(DIL_PAIRS):
        offs = (Q_OFF + g * GROUP_W, KV_OFF + g * 2 * GROUP_W, KV_OFF + g * 2 * GROUP_W + GROUP_W)
        for which, off in enumerate(offs):
            y = proj(off)
            if which > 0:
                kv_refs[g][0, (which - 1) * GROUP_W:which * GROUP_W, :] = jnp.transpose(y)
            else:
                y = y * (ATT_HD ** -0.5)
            dst = folded[g][which]
            if dil == 1:
                dst[...] = y.astype(BF16)
            else:
                for cb in range(GROUP_W // LANES):
                    y_scr[cb] = y[:, cb * LANES:(cb + 1) * LANES]
                for r in range(dil):
                    for cb in range(GROUP_W // LANES):
                        dst[0, r, :, cb * LANES:(cb + 1) * LANES] = (
                            y_scr[cb, pl.ds(r, tm // dil, stride=dil), :].astype(BF16))

    gu = _gelu(proj(GU_OFF))
    gv = _gelu(proj(GV_OFF))
    row = lax.broadcasted_iota(jnp.int32, (CHUNK, CHUNK), 0)
    col = lax.broadcasted_iota(jnp.int32, (CHUNK, CHUNK), 1)
    for hh in range(GM_HEADS):
        gvn = _gm_layernorm(gv, lng_ref, lnb_ref, hh).astype(BF16)
        w_tril = jnp.where(row >= col, ws_ref[hh], 0.0).astype(BF16)
        for ci in range(tm // CHUNK):
            z = _dot(w_tril, gvn[ci * CHUNK:(ci + 1) * CHUNK]) + bsb_ref[hh]
            gu_blk = gu[ci * CHUNK:(ci + 1) * CHUNK, hh * GM_HD:(hh + 1) * GM_HD]
            gm_ref[ci * CHUNK:(ci + 1) * CHUNK, hh * GM_HD:(hh + 1) * GM_HD] = (gu_blk * z).astype(BF16)


def _inproj_prompt(x, mod, norm_g, w_in_p, w_s, bsb, ln_g, ln_b, batch, seq):
    n = x.shape[0]
    tm = 256
    tiles_per_seq = seq // tm

    def mod_spec(k):
        return pl.BlockSpec((None, 1, D_MODEL), lambda i: ((i // tiles_per_seq) * 6 + k, 0, 0))

    def folded_spec(dil):
        if dil == 1:
            return pl.BlockSpec((tm, GROUP_W), lambda i: (i, 0))
        return pl.BlockSpec((1, dil, tm // dil, GROUP_W),
                            lambda i: (i // tiles_per_seq, 0, i % tiles_per_seq, 0))

    def folded_shape(dil):
        if dil == 1:
            return jax.ShapeDtypeStruct((n, GROUP_W), BF16)
        return jax.ShapeDtypeStruct((batch, dil, seq // dil, GROUP_W), BF16)

    out_specs, out_shape = [], []
    for _, dil in DIL_PAIRS:
        for _ in range(3):
            out_specs.append(folded_spec(dil))
            out_shape.append(folded_shape(dil))
    for _ in range(N_DIL):
        out_specs.append(pl.BlockSpec((1, 2 * GROUP_W, tm),
                                      lambda i: (i // tiles_per_seq, 0, i % tiles_per_seq)))
        out_shape.append(jax.ShapeDtypeStruct((batch, 2 * GROUP_W, seq), F32))
    out_specs.append(pl.BlockSpec((tm, GM_WIDTH), lambda i: (i, 0)))
    out_shape.append(jax.ShapeDtypeStruct((n, GM_WIDTH), BF16))

    return pl.pallas_call(
        _inproj_prompt_kernel,
        grid=(n // tm,),
        in_specs=[
            pl.BlockSpec((tm, D_MODEL), lambda i: (i, 0)),
            mod_spec(0), mod_spec(1),
            pl.BlockSpec((None, 1, D_MODEL), lambda i: (0, 0, 0)),
            pl.BlockSpec((D_MODEL, IN_WIDTH), lambda i: (0, 0)),
            pl.BlockSpec((GM_HEADS, CHUNK, CHUNK), lambda i: (0, 0, 0)),
            pl.BlockSpec((GM_HEADS, CHUNK, GM_HD), lambda i: (0, 0, 0)),
            pl.BlockSpec((GM_HEADS, GM_HD), lambda i: (0, 0)),
            pl.BlockSpec((GM_HEADS, GM_HD), lambda i: (0, 0)),
        ],
        out_specs=out_specs,
        out_shape=out_shape,
        scratch_shapes=[pltpu.VMEM((tm, D_MODEL), BF16), pltpu.VMEM((GROUP_W // LANES, tm, LANES), F32)],
        compiler_params=_cparams(("parallel",)),
        name="inproj_prompt",
    )(x, mod, mod, norm_g, w_in_p, w_s, bsb, ln_g, ln_b)


def _inproj_sample_kernel(x_ref, sh_ref, sc_ref, g_ref, w_ref, wd_ref, b0_ref, lng_ref, lnb_ref,
                          q_ref, kv0_ref, kv1_ref, kv2_ref, gm_ref, gvn_ref):
    h = _modulated_norm(x_ref, sh_ref, sc_ref, g_ref).astype(BF16)
    q_ref[...] = _dot(h, w_ref[:, Q_OFF:Q_OFF + ATT_WIDTH])
    for g, kv_ref in enumerate((kv0_ref, kv1_ref, kv2_ref)):
        off = KV_OFF + g * 2 * GROUP_W
        kv_ref[...] = _dot(h, w_ref[:, off:off + 2 * GROUP_W])
    gu = _gelu(_dot(h, w_ref[:, GU_OFF:GU_OFF + GM_WIDTH]))
    gv = _gelu(_dot(h, w_ref[:, GV_OFF:GV_OFF + GM_WIDTH]))
    for hh in range(GM_HEADS):
        sl = slice(hh * GM_HD, (hh + 1) * GM_HD)
        gvn = _gm_layernorm(gv, lng_ref, lnb_ref, hh)
        gvn_ref[:, sl] = gvn
        z = wd_ref[:, sl] * gvn + b0_ref[:, sl]
        gm_ref[:, sl] = (gu[:, sl] * z).astype(BF16)


def _inproj_sample(x, sh, sc, norm_g, w_in_p, wd, b0, ln_g, ln_b):
    n = x.shape[0]
    full = lambda shape: pl.BlockSpec(shape, lambda i: tuple(0 for _ in shape))
    return pl.pallas_call(
        _inproj_sample_kernel,
        grid=(1,),
        in_specs=[
            full((n, D_MODEL)), full((n, D_MODEL)), full((n, D_MODEL)),
            pl.BlockSpec((None, 1, D_MODEL), lambda i: (0, 0, 0)),
            full((D_MODEL, IN_WIDTH)),
            full((1, GM_WIDTH)), full((1, GM_WIDTH)),
            full((GM_HEADS, GM_HD)), full((GM_HEADS, GM_HD)),
        ],
        out_specs=[full((n, ATT_WIDTH)), full((n, 2 * GROUP_W)), full((n, 2 * GROUP_W)),
                   full((n, 2 * GROUP_W)), full((n, GM_WIDTH)), full((n, GM_WIDTH))],
        out_shape=[jax.ShapeDtypeStruct((n, ATT_WIDTH), F32)]
        + [jax.ShapeDtypeStruct((n, 2 * GROUP_W), F32)] * 3
        + [jax.ShapeDtypeStruct((n, GM_WIDTH), BF16), jax.ShapeDtypeStruct((n, GM_WIDTH), F32)],
        compiler_params=_cparams(("arbitrary",)),
        name="inproj_sample",
    )(x, sh, sc, norm_g, w_in_p, wd, b0, ln_g, ln_b)


def _attn_block(qt, kt, vt, off, nkeys):
    lane = lax.broadcasted_iota(jnp.int32, (1, LANES), 1)
    lo = lane < ATT_HD
    qi = lax.broadcasted_iota(jnp.int32, (QBLK, nkeys), 0)
    kk = lax.broadcasted_iota(jnp.int32, (QBLK, nkeys), 1)
    dist = qi - kk + off
    valid = lax.bitcast_convert_type(dist, jnp.uint32) <= jnp.uint32(BAND)
    res = []
    for sel in (lo, jnp.logical_not(lo)):
        qm = jnp.where(sel, qt, jnp.zeros_like(qt))
        s = jnp.where(valid, _dot_nt(qm, kt), NEG_INF)
        m = jnp.max(s, axis=-1, keepdims=True)
        p = jnp.exp(s - m)
        den = jnp.sum(p, axis=-1, keepdims=True)
        o = _dot(p.astype(BF16), vt) * (1.0 / den)
        res.append((o, m + jnp.log(den)))
    o = jnp.where(lo, res[0][0], res[1][0])
    lse = jnp.where(lo, res[0][1], res[1][1])
    return o, lse


def _attn_prompt_kernel(q0_ref, k0_ref, v0_ref, q1_ref, k1_ref, v1_ref, q2_ref, k2_ref, v2_ref,
                        a0_ref, a1_ref, a2_ref, of_scr, lf_scr, to_scr, tl_scr):
    seq = q0_ref.shape[0]

    def run_block(g, qt, kt, vt, off, nkeys, row0):
        o, lse = _attn_block(qt, kt, vt, off, nkeys)
        of_scr[g, pl.ds(row0, QBLK), :] = o
        lf_scr[g, pl.ds(row0, QBLK), :] = lse

    par = 16

    def g0_body(it, carry):
        for u in range(par):
            qb = it * par + u
            qs = pl.multiple_of(qb * QBLK, QBLK)
            ks = pl.multiple_of(jnp.maximum(qb - 1, 0) * QBLK, QBLK)
            run_block(0, q0_ref[pl.ds(qs, QBLK), :], k0_ref[pl.ds(ks, 2 * QBLK), :],
                      v0_ref[pl.ds(ks, 2 * QBLK), :], qs - ks, 2 * QBLK, qs)
        return carry

    lax.fori_loop(0, seq // QBLK // par, g0_body, 0)

    dil1 = DIL_PAIRS[1][1]
    len1 = seq // dil1

    res_per_body = max(par // (len1 // QBLK), 1)

    def g1_body(it, carry):
        for u in range(res_per_body):
            r = it * res_per_body + u
            for qb in range(len1 // QBLK):
                qs = qb * QBLK
                ks = max(qb - 1, 0) * QBLK
                run_block(1, q1_ref[0, r, pl.ds(qs, QBLK), :], k1_ref[0, r, pl.ds(ks, 2 * QBLK), :],
                          v1_ref[0, r, pl.ds(ks, 2 * QBLK), :], qs - ks, 2 * QBLK,
                          pl.multiple_of(r * len1 + qs, QBLK))
        return carry

    lax.fori_loop(0, dil1 // res_per_body, g1_body, 0)

    dil2 = DIL_PAIRS[2][1]
    len2 = seq // dil2

    def g2_body(it, carry):
        for u in range(par):
            r = it * par + u
            pad = it * par + (u ^ 1)
            kt = jnp.concatenate([k2_ref[0, r], k2_ref[0, pad]], axis=0)
            vt = jnp.concatenate([v2_ref[0, r], v2_ref[0, pad]], axis=0)
            run_block(2, q2_ref[0, r], kt, vt, 0, 2 * QBLK, pl.multiple_of(r * len2, QBLK))
        return carry

    lax.fori_loop(0, dil2 // par, g2_body, 0)

    def mix_body(nt, carry):
        t0 = pl.multiple_of(nt * QBLK, QBLK)
        o_nat = [of_scr[0, pl.ds(t0, QBLK), :]]
        l_nat = [lf_scr[0, pl.ds(t0, QBLK), :]]
        for g, dil, length in ((1, dil1, len1), (2, dil2, len2)):
            per = QBLK // dil
            for r in range(dil):
                src = pl.multiple_of(r * length + nt * per, per)
                to_scr[g - 1, pl.ds(r, per, stride=dil), :] = of_scr[g, pl.ds(src, per), :]
                tl_scr[g - 1, pl.ds(r, per, stride=dil), :] = lf_scr[g, pl.ds(src, per), :]
            o_nat.append(to_scr[g - 1])
            l_nat.append(tl_scr[g - 1])
        mx = jnp.maximum(jnp.maximum(l_nat[0], l_nat[1]), l_nat[2])
        e = [jnp.exp(l - mx) for l in l_nat]
        inv = 1.0 / (e[0] + e[1] + e[2])
        for g, a_ref in enumerate((a0_ref, a1_ref, a2_ref)):
            a_ref[pl.ds(t0, QBLK), :] = (o_nat[g] * (e[g] * inv)).astype(BF16)
        return carry

    lax.fori_loop(0, seq // QBLK, mix_body, 0)


def _attn_prompt(folded, batch, seq):
    n = batch * seq
    pairs = GROUP_W // LANES
    in_specs = []
    for _, dil in DIL_PAIRS:
        for _ in range(3):
            if dil == 1:
                in_specs.append(pl.BlockSpec((seq, LANES), lambda b, hp: (b, hp)))
            else:
                in_specs.append(pl.BlockSpec((1, dil, seq // dil, LANES), lambda b, hp: (b, 0, 0, hp)))
    return pl.pallas_call(
        _attn_prompt_kernel,
        grid=(batch, pairs),
        in_specs=in_specs,
        out_specs=[pl.BlockSpec((seq, LANES), lambda b, hp: (b, hp))] * N_DIL,
        out_shape=[jax.ShapeDtypeStruct((n, GROUP_W), BF16)] * N_DIL,
        scratch_shapes=[pltpu.VMEM((N_DIL, seq, LANES), F32), pltpu.VMEM((N_DIL, seq, LANES), F32),
                        pltpu.VMEM((N_DIL - 1, QBLK, LANES), F32), pltpu.VMEM((N_DIL - 1, QBLK, LANES), F32)],
        compiler_params=_cparams(("parallel", "parallel")),
        name="attn_prompt",
    )(*folded)


def _attn_sample_kernel(q_ref, n0_ref, n1_ref, n2_ref, c0_ref, c1_ref, c2_ref, o_ref):
    groups = ((n0_ref, c0_ref), (n1_ref, c1_ref), (n2_ref, c2_ref))

    def head(h, carry):
        outs, lses = [], []
        for g, (n_ref, c_ref) in enumerate(groups):
            dil = DIL_PAIRS[g][1]
            nbuf = c_ref.shape[-1]
            q = q_ref[0, g, h] * (ATT_HD ** -0.5)
            row = lax.broadcasted_iota(jnp.int32, (1, nbuf), 1)
            valid = jnp.bitwise_and(row, dil - 1) == 0
            s_buf = jnp.sum(c_ref[0, 0, h] * q, axis=0, keepdims=True)
            s_buf = jnp.where(valid, s_buf, NEG_INF)
            s_new = jnp.sum(n_ref[0, 0, h] * q, axis=0, keepdims=True)
            m = jnp.maximum(jnp.max(s_buf, axis=1, keepdims=True), s_new)
            p_buf = jnp.exp(s_buf - m)
            p_new = jnp.exp(s_new - m)
            den = jnp.sum(p_buf, axis=1, keepdims=True) + p_new
            o = jnp.sum(c_ref[0, 1, h] * p_buf, axis=1, keepdims=True) + n_ref[0, 1, h] * p_new
            outs.append(o / den)
            lses.append(m + jnp.log(den))
        mx = jnp.maximum(jnp.maximum(lses[0], lses[1]), lses[2])
        e = [jnp.exp(l - mx) for l in lses]
        inv = 1.0 / (e[0] + e[1] + e[2])
        for g in range(N_DIL):
            o_ref[0, g, h] = outs[g] * (e[g] * inv)
        return carry

    lax.fori_loop(0, ATT_HEADS, head, 0)


def _attn_sample(q, new_kvs, caches):
    n_real = q.shape[0]
    assert all(dil & (dil - 1) == 0 and c.shape[-1] == BAND * dil for c, (_, dil) in zip(caches, DIL_PAIRS))
    per_row = lambda a: pl.BlockSpec((1,) + a.shape[1:], lambda b: (b,) + (0,) * (a.ndim - 1))
    return pl.pallas_call(
        _attn_sample_kernel,
        grid=(n_real,),
        in_specs=[per_row(a) for a in (q, *new_kvs, *caches)],
        out_specs=per_row(q),
        out_shape=jax.ShapeDtypeStruct(q.shape, F32),
        compiler_params=_cparams(("arbitrary",)),
        name="attn_sample",
    )(q, *new_kvs, *caches)


SUBLANES = 8


def _oddeven_merge_sort_pairs(n):
    pairs = []
    p = 1
    while p < n:
        k = p
        while k >= 1:
            for j in range(k % p, n - k, 2 * k):
                for i in range(min(k, n - j - k)):
                    if (i + j) // (2 * p) == (i + j + k) // (2 * p):
                        pairs.append((i + j, i + j + k))
            k //= 2
        p *= 2
    return pairs


def _topk_desc(sc, k):
    n = sc.shape[0] // SUBLANES
    rows = [sc[SUBLANES * i:SUBLANES * (i + 1)] for i in range(n)]
    for i, j in _oddeven_merge_sort_pairs(n):
        rows[i], rows[j] = jnp.maximum(rows[i], rows[j]), jnp.minimum(rows[i], rows[j])
    vals = []
    for it in range(k):
        m = jnp.max(rows[0], axis=0, keepdims=True)
        vals.append(m)
        if it + 1 < k:
            win = rows[0] == m
            depth = min(n, k - it)
            for r in range(depth - 1):
                rows[r] = jnp.where(win, rows[r + 1], rows[r])
            rows[depth - 1] = jnp.where(win, NEG_INF, rows[depth - 1])
    return vals


def _outproj_peer_kernel(x_ref, a0_ref, a1_ref, a2_ref, gm_ref, gt1_ref, sh2_ref, sc2_ref, g1_ref, g2_ref,
                         wo_ref, wq_ref, keys_ref,
                         x1_ref, h2t_ref, s1_ref, s2_ref, e2_ref, thr_ref, lse_ref):
    mo = _dot(a0_ref[...], wo_ref[0:GROUP_W, :])
    mo += _dot(a1_ref[...], wo_ref[GROUP_W:2 * GROUP_W, :])
    mo += _dot(a2_ref[...], wo_ref[2 * GROUP_W:3 * GROUP_W, :])
    mo += _dot(gm_ref[...], wo_ref[3 * GROUP_W:4 * GROUP_W, :])
    x1 = x_ref[...] + gt1_ref[...] * _rms(mo, g1_ref[...])
    x1_ref[...] = x1
    h2 = _rms(x1, g2_ref[...]) * (1.0 + sc2_ref[...]) + sh2_ref[...]
    h2t_ref[0] = jnp.transpose(h2).astype(BF16)
    qr = _dot(h2.astype(BF16), wq_ref[...])
    lane_groups = thr_ref.shape[0]
    for h in range(PEER_HEADS):
        tops = []
        for p, s_ref in enumerate((s1_ref, s2_ref)):
            c0 = (h * 2 + p) * PEER_HALF
            sc = _dot_nt(keys_ref[p], qr[:, c0:c0 + PEER_HALF].astype(BF16)) * LOG2E
            for lg in range(lane_groups):
                s_ref[lg, h] = sc[:, lg * LANES:(lg + 1) * LANES]
            tops.append(_topk_desc(sc, PEER_TOPK))
            if p == 1:
                e2 = jnp.exp2(sc - tops[1][0])
                for lg in range(lane_groups):
                    e2_ref[lg, h] = e2[:, lg * LANES:(lg + 1) * LANES]
        t1 = jnp.concatenate(tops[0], axis=0)
        half = PEER_TOPK // 2
        cand = jnp.concatenate(
            [t1 + tops[1][0]]
            + [t1[:half] + tops[1][b] for b in range(1, half)]
            + [jnp.concatenate(tops[1][half:], axis=0) + tops[0][0]], axis=0)
        best = _topk_desc(cand, PEER_TOPK)
        z = jnp.ones_like(best[0])
        for v in best[1:]:
            z = z + jnp.exp2(v - best[0])
        lse = best[0] + jnp.log(z) * LOG2E - tops[1][0]
        for lg in range(lane_groups):
            thr_ref[lg, pl.ds(h, 1), :] = best[-1][:, lg * LANES:(lg + 1) * LANES]
            lse_ref[lg, pl.ds(h, 1), :] = lse[:, lg * LANES:(lg + 1) * LANES]


def _outproj_peer(x, atts, gm, mods, norm_g, w_out, w_q, keys, tm, per_row_mod, tiles_per_seq):
    n = x.shape[0]
    if per_row_mod:
        mod_specs = [pl.BlockSpec((tm, D_MODEL), lambda i: (i, 0))] * 3
    else:
        mod_specs = [pl.BlockSpec((None, 1, D_MODEL), lambda i, k=k: ((i // tiles_per_seq) * 6 + k, 0, 0))
                     for k in (2, 3, 4)]
    row_spec = lambda w: pl.BlockSpec((tm, w), lambda i: (i, 0))
    const = lambda shape: pl.BlockSpec(shape, lambda i: tuple(0 for _ in shape))
    return pl.pallas_call(
        _outproj_peer_kernel,
        grid=(n // tm,),
        in_specs=[row_spec(D_MODEL)] + [row_spec(GROUP_W)] * 4 + mod_specs
        + [pl.BlockSpec((None, 1, D_MODEL), lambda i: (1, 0, 0)),
           pl.BlockSpec((None, 1, D_MODEL), lambda i: (2, 0, 0)),
           const((D_MODEL, D_MODEL)), const((D_MODEL, D_MODEL)),
           const((2, N_KEYS, PEER_HALF))],
        out_specs=[row_spec(D_MODEL),
                   pl.BlockSpec((1, D_MODEL, tm), lambda i: (i, 0, 0)),
                   pl.BlockSpec((tm // LANES, PEER_HEADS, N_KEYS, LANES), lambda i: (i, 0, 0, 0)),
                   pl.BlockSpec((tm // LANES, PEER_HEADS, N_KEYS, LANES), lambda i: (i, 0, 0, 0)),
                   pl.BlockSpec((tm // LANES, PEER_HEADS, N_KEYS, LANES), lambda i: (i, 0, 0, 0)),
                   pl.BlockSpec((tm // LANES, PEER_HEADS, LANES), lambda i: (i, 0, 0)),
                   pl.BlockSpec((tm // LANES, PEER_HEADS, LANES), lambda i: (i, 0, 0))],
        out_shape=[jax.ShapeDtypeStruct((n, D_MODEL), F32),
                   jax.ShapeDtypeStruct((n // tm, D_MODEL, tm), BF16),
                   jax.ShapeDtypeStruct((n // LANES, PEER_HEADS, N_KEYS, LANES), F32),
                   jax.ShapeDtypeStruct((n // LANES, PEER_HEADS, N_KEYS, LANES), F32),
                   jax.ShapeDtypeStruct((n // LANES, PEER_HEADS, N_KEYS, LANES), F32),
                   jax.ShapeDtypeStruct((n // LANES, PEER_HEADS, LANES), F32),
                   jax.ShapeDtypeStruct((n // LANES, PEER_HEADS, LANES), F32)],
        compiler_params=_cparams(("parallel",)),
        name="outproj_peer",
    )(x, *atts, gm, *mods, norm_g, norm_g, w_out, w_q, keys)


def _expert_kernel(h2t_ref, u_ref, vt_ref, s1_ref, s2_ref, e2_ref, thr_ref, lse_ref, x1_ref, gt2_ref, g3_ref,
                   y_ref, acc_scr, g_scr, w_scr, rowb_scr, thrb_scr, tile_scr):
    j = pl.program_id(1)
    n_tiles = pl.num_programs(1) - 1
    te = u_ref.shape[0]
    rows_per_tile = te // N_KEYS
    lane_groups = s1_ref.shape[0]
    n_chunks, _, chunk_w = h2t_ref.shape
    lg_per_chunk = chunk_w // LANES
    n_rows = rows_per_tile * PEER_HEADS
    ch = 8

    @pl.when(j == 0)
    def _():
        acc_scr[...] = jnp.zeros_like(acc_scr)
        g_scr[...] = jnp.zeros_like(g_scr)
        for lg in range(lane_groups):
            for h in range(PEER_HEADS):
                thrb_scr[lg, h] = jnp.broadcast_to(thr_ref[lg, pl.ds(h, 1), :], (ch, LANES))

    jt = jnp.minimum(j, n_tiles - 1)
    for lg in range(lane_groups):
        for rr in range(rows_per_tile):
            tile = tile_scr.at[lg * rows_per_tile + rr]
            for h in range(PEER_HEADS):
                tile[pl.ds(h, 1), :] = s1_ref[lg, h, pl.ds(jt * rows_per_tile + rr, 1), :]
            rows8 = tile[...]
            e8 = jnp.exp2(rows8 - lse_ref[lg])
            for h in range(PEER_HEADS):
                rowb_scr[lg, rr * PEER_HEADS + h] = jnp.broadcast_to(rows8[h:h + 1], (ch, LANES))
                rowb_scr[lg, n_rows + rr * PEER_HEADS + h] = jnp.broadcast_to(e8[h:h + 1], (ch, LANES))

    def build_gates(lg):
        for cc in range(N_KEYS // ch):
            gates = [None] * rows_per_tile
            for h in range(PEER_HEADS):
                b = s2_ref[lg, h, cc * ch:(cc + 1) * ch, :]
                eb = e2_ref[lg, h, cc * ch:(cc + 1) * ch, :]
                t = thrb_scr[lg, h]
                for rr in range(rows_per_tile):
                    a = rowb_scr[lg, rr * PEER_HEADS + h]
                    ea = rowb_scr[lg, n_rows + rr * PEER_HEADS + h]
                    term = jnp.where(a + b >= t, ea * eb, 0.0)
                    gates[rr] = term if gates[rr] is None else gates[rr] + term
            for rr in range(rows_per_tile):
                g_scr[lg, rr * N_KEYS + cc * ch:rr * N_KEYS + (cc + 1) * ch, :] = gates[rr]

    for k in range(n_chunks):
        act = _dot(u_ref[...], h2t_ref[k])
        for q in range(lg_per_chunk):
            lg = k * lg_per_chunk + q
            w_scr[:, lg * LANES:(lg + 1) * LANES] = (
                g_scr[lg] * _gelu_sigmoid(act[:, q * LANES:(q + 1) * LANES])).astype(BF16)

    d_slab = acc_scr.shape[0] // lane_groups

    def piece(lg, carry):
        d0 = pl.multiple_of(lg * d_slab, d_slab)
        acc_scr[pl.ds(d0, d_slab), :] += _dot(vt_ref[pl.ds(d0, d_slab), :], w_scr[...])
        build_gates(lg)
        return carry

    lax.fori_loop(0, lane_groups, piece, 0)

    @pl.when(j == n_tiles)
    def _():
        peer = jnp.transpose(acc_scr[...])
        y_ref[...] = x1_ref[...] + gt2_ref[...] * _rms(peer, g3_ref[...])


def _experts(h2t, u_bf, vt_bf, s1, s2, e2, thr, lse, x1, gt2, norm_g, tm, per_row_mod, tiles_per_seq):
    chunk_w = h2t.shape[2]
    n = h2t.shape[0] * chunk_w
    te = 512
    n_tiles = N_EXPERTS // te
    prev_tile = lambda j: jnp.maximum(j - 1, 0)
    if per_row_mod:
        gt2_spec = pl.BlockSpec((tm, D_MODEL), lambda i, j: (i, 0))
    else:
        gt2_spec = pl.BlockSpec((None, 1, D_MODEL), lambda i, j: ((i // tiles_per_seq) * 6 + 5, 0, 0))
    return pl.pallas_call(
        _expert_kernel,
        grid=(n // tm, n_tiles + 1),
        in_specs=[
            pl.BlockSpec((tm // chunk_w, D_MODEL, chunk_w), lambda i, j: (i, 0, 0)),
            pl.BlockSpec((te, D_MODEL), lambda i, j: (prev_tile(j), 0)),
            pl.BlockSpec((D_MODEL, te), lambda i, j: (0, prev_tile(j))),
            pl.BlockSpec((tm // LANES, PEER_HEADS, N_KEYS, LANES), lambda i, j: (i, 0, 0, 0)),
            pl.BlockSpec((tm // LANES, PEER_HEADS, N_KEYS, LANES), lambda i, j: (i, 0, 0, 0)),
            pl.BlockSpec((tm // LANES, PEER_HEADS, N_KEYS, LANES), lambda i, j: (i, 0, 0, 0)),
            pl.BlockSpec((tm // LANES, PEER_HEADS, LANES), lambda i, j: (i, 0, 0)),
            pl.BlockSpec((tm // LANES, PEER_HEADS, LANES), lambda i, j: (i, 0, 0)),
            pl.BlockSpec((tm, D_MODEL), lambda i, j: (i, 0)),
            gt2_spec,
            pl.BlockSpec((None, 1, D_MODEL), lambda i, j: (3, 0, 0)),
        ],
        out_specs=pl.BlockSpec((tm, D_MODEL), lambda i, j: (i, 0)),
        out_shape=jax.ShapeDtypeStruct((n, D_MODEL), F32),
        scratch_shapes=[pltpu.VMEM((D_MODEL, tm), F32), pltpu.VMEM((tm // LANES, te, LANES), F32),
                        pltpu.VMEM((te, tm), BF16),
                        pltpu.VMEM((tm // LANES, 2 * (te // N_KEYS) * PEER_HEADS, 8, LANES), F32),
                        pltpu.VMEM((tm // LANES, PEER_HEADS, 8, LANES), F32),
                        pltpu.VMEM((tm // LANES * (te // N_KEYS), PEER_HEADS, LANES), F32)],
        compiler_params=_cparams(("parallel", "arbitrary")),
        name="peer_experts",
    )(h2t, u_bf, vt_bf, s1, s2, e2, thr, lse, x1, gt2, norm_g)


def _cast_transpose_kernel(v_ref, o_ref):
    o_ref[...] = jnp.transpose(v_ref[...]).astype(BF16)


def _cast_transpose(v):
    rows, cols = v.shape
    tr = 1024
    return pl.pallas_call(
        _cast_transpose_kernel,
        grid=(rows // tr,),
        in_specs=[pl.BlockSpec((tr, cols), lambda i: (i, 0))],
        out_specs=pl.BlockSpec((cols, tr), lambda i: (0, i)),
        out_shape=jax.ShapeDtypeStruct((cols, rows), BF16),
        compiler_params=_cparams(("parallel",)),
        name="cast_transpose",
    )(v)


def _permute_w_in(w_in):
    q = w_in[:, :ATT_WIDTH]
    k = w_in[:, ATT_WIDTH:2 * ATT_WIDTH]
    v = w_in[:, 2 * ATT_WIDTH:3 * ATT_WIDTH]
    parts = [q]
    for g in range(N_DIL):
        parts += [k[:, g * GROUP_W:(g + 1) * GROUP_W], v[:, g * GROUP_W:(g + 1) * GROUP_W]]
    parts.append(w_in[:, 3 * ATT_WIDTH:])
    return jnp.concatenate(parts, axis=1).astype(BF16)


def kernel(x_prompt, x_sample, c_prompt, c_sample, cache_kv_dil1, cache_kv_dil4, cache_kv_dil16, w_ada, b_ada, norm_g, w_in, gm_ln_g, gm_ln_b, w_s, b_s, w_out, w_q, sub_keys, expert_u, expert_v):
    batch, seq, _ = x_prompt.shape
    n_s = x_sample.shape[0] * x_sample.shape[1]
    assert w_ada.shape[0] == 1 and x_sample.shape[1] == 1
    n_p = batch * seq
    pad_s = 128

    w_in_p = _permute_w_in(w_in[0])
    w_out_b = w_out[0].astype(BF16)
    w_q_b = w_q[0].astype(BF16)
    keys_b = sub_keys[0].astype(BF16)
    u_b = expert_u[0].astype(BF16)
    vt_b = _cast_transpose(expert_v[0])
    ng = norm_g[0].reshape(4, 1, D_MODEL)
    ln_g, ln_b = gm_ln_g[0], gm_ln_b[0]

    rows = batch + n_s
    c_all = jnp.zeros((16, D_MODEL), F32).at[:rows].set(jnp.concatenate([c_prompt, c_sample], axis=0))
    mod = _ada(c_all, w_ada[0], b_ada)
    mod_p = mod.reshape(16 * 6, 1, D_MODEL)
    mod_s = jnp.zeros((pad_s, 6, D_MODEL), F32).at[:n_s].set(mod[batch:rows].reshape(n_s, 6, D_MODEL))
    mods_s = [mod_s[:, k] for k in range(6)]

    xp = x_prompt.reshape(n_p, D_MODEL)
    bsb = jnp.broadcast_to(b_s[0][:, :, None], (GM_HEADS, CHUNK, GM_HD))
    outs = _inproj_prompt(xp, mod_p, ng, w_in_p, w_s[0], bsb, ln_g, ln_b, batch, seq)
    folded, kv_p, gm_p = outs[:9], outs[9:12], outs[12]
    att_p = _attn_prompt(folded, batch, seq)
    tm_p = 256
    x1_p, h2t_p, *sel_p = _outproj_peer(
        xp, att_p, gm_p, [mod_p] * 3, ng, w_out_b, w_q_b, keys_b, tm_p, False, seq // tm_p)
    tm_e = 512
    y_p = _experts(h2t_p, u_b, vt_b, *sel_p, x1_p, mod_p, ng, tm_e, False, seq // tm_e)

    xs = jnp.zeros((pad_s, D_MODEL), F32).at[:n_s].set(x_sample.reshape(n_s, D_MODEL))
    wd = jnp.repeat(w_s[0][:, 0, 0], GM_HD)[None, :]
    b0 = jnp.repeat(b_s[0][:, 0], GM_HD)[None, :]
    q_s, kv0_s, kv1_s, kv2_s, gm_s, gvn_s = _inproj_sample(
        xs, mods_s[0], mods_s[1], ng, w_in_p, wd, b0, ln_g, ln_b)
    new_kv_s = [kv[:n_s].reshape(n_s, 2, ATT_HEADS, ATT_HD) for kv in (kv0_s, kv1_s, kv2_s)]
    caches_t = [jnp.transpose(c[0], (0, 2, 3, 4, 1)) for c in (cache_kv_dil1, cache_kv_dil4, cache_kv_dil16)]
    att_s = _attn_sample(q_s[:n_s].reshape(n_s, N_DIL, ATT_HEADS, ATT_HD, 1),
                         [kv[..., None] for kv in new_kv_s], caches_t)
    att_s = att_s.reshape(n_s, N_DIL, GROUP_W).astype(BF16)
    att_s = [jnp.zeros((pad_s, GROUP_W), BF16).at[:n_s].set(att_s[:, g]) for g in range(N_DIL)]
    x1_s, h2t_s, *sel_s = _outproj_peer(
        xs, att_s, gm_s, mods_s[2:5], ng, w_out_b, w_q_b, keys_b, pad_s, True, 1)
    y_s = _experts(h2t_s, u_b, vt_b, *sel_s, x1_s, mods_s[5], ng, pad_s, True, 1)

    y_prompt = y_p.reshape(batch, seq, D_MODEL)
    y_sample = y_s[:n_s].reshape(x_sample.shape)
    new_kv_p = []
    for kv_t, (win, _) in zip(kv_p, DIL_PAIRS):
        keep = min(win, seq)
        kv_t = kv_t[:, :, seq - keep:].reshape(batch, 2, ATT_HEADS, ATT_HD, keep)
        new_kv_p.append(jnp.transpose(kv_t, (0, 4, 1, 2, 3))[None])
    new_kv_s = [kv.reshape(1, n_s, 1, 2, ATT_HEADS, ATT_HD) for kv in new_kv_s]
    state_gv = gvn_s[:n_s].reshape(1, n_s, 1, GM_HEADS, GM_HD)
    return (y_prompt, y_sample, *new_kv_p, *new_kv_s, state_gv)
```

```python
import functools
import math

import jax
import jax.numpy as jnp
from jax import lax
from jax.experimental import pallas as pl
from jax.experimental.pallas import tpu as pltpu

F32 = jnp.float32
BF16 = jnp.bfloat16

D_MODEL = 2048
ATT_HD = 64
ATT_HEADS = 8
GROUP_W = ATT_HEADS * ATT_HD
DIL_PAIRS = ((128, 1), (512, 4), (2048, 16))
N_DIL = 3
ATT_WIDTH = N_DIL * GROUP_W
QBLK = 128
BAND = 128
CHUNK = 128
GM_HEADS = 4
GM_HD = 128
GM_WIDTH = GM_HEADS * GM_HD
IN_WIDTH = 3 * ATT_WIDTH + 2 * GM_WIDTH
N_KEYS = 128
N_EXPERTS = N_KEYS * N_KEYS
PEER_HEADS = 8
PEER_HALF = 128
PEER_TOPK = 16
EPS = 1e-6
NEG_INF = float("-inf")
LOG2E = 1.4426950408889634

LANES = 128
VMEM_LIMIT = 56 * 1024 * 1024

Q_OFF = 0
KV_OFF = ATT_WIDTH
GU_OFF = ATT_WIDTH + N_DIL * 2 * GROUP_W
GV_OFF = GU_OFF + GM_WIDTH


def _cparams(sem, flags=None):
    return pltpu.CompilerParams(dimension_semantics=sem, vmem_limit_bytes=VMEM_LIMIT, flags=flags)


def _gelu(x):
    c = math.sqrt(2.0 / math.pi)
    return 0.5 * x * (1.0 + jnp.tanh(c * (x + 0.044715 * (x * x * x))))


def _gelu_sigmoid(x):
    k2 = -2.0 * math.sqrt(2.0 / math.pi) * LOG2E
    k1 = k2 * 0.044715
    return x / (1.0 + jnp.exp2(x * (k2 + k1 * (x * x))))


def _rms(x, g):
    return x * lax.rsqrt(jnp.mean(x * x, axis=-1, keepdims=True) + EPS) * g


def _dot(a, b):
    return jnp.dot(a, b, preferred_element_type=F32)


def _dot_nt(a, b):
    return lax.dot_general(a, b, (((1,), (1,)), ((), ())), preferred_element_type=F32)


def _ada_kernel(c_ref, w_ref, b_ref, o_ref):
    c = c_ref[...]
    act = (c / (1.0 + jnp.exp(-c))).astype(BF16)
    o_ref[...] = _dot(act, w_ref[...].astype(BF16)) + b_ref[...]


def _ada(c_all, w_ada, b_ada):
    rows = c_all.shape[0]
    n_out = w_ada.shape[1]
    tn = 1024
    return pl.pallas_call(
        _ada_kernel,
        grid=(n_out // tn,),
        in_specs=[
            pl.BlockSpec((rows, D_MODEL), lambda j: (0, 0)),
            pl.BlockSpec((D_MODEL, tn), lambda j: (0, j)),
            pl.BlockSpec((1, tn), lambda j: (0, j)),
        ],
        out_specs=pl.BlockSpec((rows, tn), lambda j: (0, j)),
        out_shape=jax.ShapeDtypeStruct((rows, n_out), F32),
        compiler_params=_cparams(("parallel",)),
        name="ada_mod",
    )(c_all, w_ada, b_ada)


def _modulated_norm(x_ref, sh_ref, sc_ref, g_ref):
    x = x_ref[...]
    return _rms(x, g_ref[...]) * (1.0 + sc_ref[...]) + sh_ref[...]


def _gm_layernorm(gv, lng_ref, lnb_ref, hh):
    v = gv[:, hh * GM_HD:(hh + 1) * GM_HD]
    mu = jnp.mean(v, axis=-1, keepdims=True)
    var = jnp.mean(jnp.square(v - mu), axis=-1, keepdims=True)
    return (v - mu) * lax.rsqrt(var + EPS) * lng_ref[pl.ds(hh, 1), :] + lnb_ref[pl.ds(hh, 1), :]


def _inproj_prompt_kernel(x_ref, sh_ref, sc_ref, g_ref, w_ref, ws_ref, bsb_ref, lng_ref, lnb_ref,
                          q0_ref, k0_ref, v0_ref, q1_ref, k1_ref, v1_ref, q2_ref, k2_ref, v2_ref,
                          kv0_ref, kv1_ref, kv2_ref, gm_ref, h_scr, y_scr):
    tm = x_ref.shape[0]
    h_scr[...] = _modulated_norm(x_ref, sh_ref, sc_ref, g_ref).astype(BF16)

    def proj(off):
        return _dot(h_scr[...], w_ref[:, off:off + GROUP_W])

    folded = ((q0_ref, k0_ref, v0_ref), (q1_ref, k1_ref, v1_ref), (q2_ref, k2_ref, v2_ref))
    kv_refs = (kv0_ref, kv1_ref, kv2_ref)
    for g, (_, dil) in enumerate(DIL_PAIRS):
        offs = (Q_OFF + g * GROUP_W, KV_OFF + g * 2 * GROUP_W, KV_OFF + g * 2 * GROUP_W + GROUP_W)
        for which, off in enumerate(offs):
            y = proj(off)
            if which > 0:
                kv_refs[g][0, (which - 1) * GROUP_W:which * GROUP_W, :] = jnp.transpose(y)
            else:
                y = y * (ATT_HD ** -0.5)
            dst = folded[g][which]
            if dil == 1:
                dst[...] = y.astype(BF16)
            else:
                for cb in range(GROUP_W // LANES):
                    y_scr[cb] = y[:, cb * LANES:(cb + 1) * LANES]
                for r in range(dil):
                    for cb in range(GROUP_W // LANES):
                        dst[0, r, :, cb * LANES:(cb + 1) * LANES] = (
                            y_scr[cb, pl.ds(r, tm // dil, stride=dil), :].astype(BF16))

    gu = _gelu(proj(GU_OFF))
    gv = _gelu(proj(GV_OFF))
    row = lax.broadcasted_iota(jnp.int32, (CHUNK, CHUNK), 0)
    col = lax.broadcasted_iota(jnp.int32, (CHUNK, CHUNK), 1)
    for hh in range(GM_HEADS):
        gvn = _gm_layernorm(gv, lng_ref, lnb_ref, hh).astype(BF16)
        w_tril = jnp.where(row >= col, ws_ref[hh], 0.0).astype(BF16)
        for ci in range(tm // CHUNK):
            z = _dot(w_tril, gvn[ci * CHUNK:(ci + 1) * CHUNK]) + bsb_ref[hh]
            gu_blk = gu[ci * CHUNK:(ci + 1) * CHUNK, hh * GM_HD:(hh + 1) * GM_HD]
            gm_ref[ci * CHUNK:(ci + 1) * CHUNK, hh * GM_HD:(hh + 1) * GM_HD] = (gu_blk * z).astype(BF16)


def _inproj_prompt(x, mod, norm_g, w_in_p, w_s, bsb, ln_g, ln_b, batch, seq):
    n = x.shape[0]
    tm = 256
    tiles_per_seq = seq // tm

    def mod_spec(k):
        return pl.BlockSpec((None, 1, D_MODEL), lambda i: ((i // tiles_per_seq) * 6 + k, 0, 0))

    def folded_spec(dil):
        if dil == 1:
            return pl.BlockSpec((tm, GROUP_W), lambda i: (i, 0))
        return pl.BlockSpec((1, dil, tm // dil, GROUP_W),
                            lambda i: (i // tiles_per_seq, 0, i % tiles_per_seq, 0))

    def folded_shape(dil):
        if dil == 1:
            return jax.ShapeDtypeStruct((n, GROUP_W), BF16)
        return jax.ShapeDtypeStruct((batch, dil, seq // dil, GROUP_W), BF16)

    out_specs, out_shape = [], []
    for _, dil in DIL_PAIRS:
        for _ in range(3):
            out_specs.append(folded_spec(dil))
            out_shape.append(folded_shape(dil))
    for _ in range(N_DIL):
        out_specs.append(pl.BlockSpec((1, 2 * GROUP_W, tm),
                                      lambda i: (i // tiles_per_seq, 0, i % tiles_per_seq)))
        out_shape.append(jax.ShapeDtypeStruct((batch, 2 * GROUP_W, seq), F32))
    out_specs.append(pl.BlockSpec((tm, GM_WIDTH), lambda i: (i, 0)))
    out_shape.append(jax.ShapeDtypeStruct((n, GM_WIDTH), BF16))

    return pl.pallas_call(
        _inproj_prompt_kernel,
        grid=(n // tm,),
        in_specs=[
            pl.BlockSpec((tm, D_MODEL), lambda i: (i, 0)),
            mod_spec(0), mod_spec(1),
            pl.BlockSpec((None, 1, D_MODEL), lambda i: (0, 0, 0)),
            pl.BlockSpec((D_MODEL, IN_WIDTH), lambda i: (0, 0)),
            pl.BlockSpec((GM_HEADS, CHUNK, CHUNK), lambda i: (0, 0, 0)),
            pl.BlockSpec((GM_HEADS, CHUNK, GM_HD), lambda i: (0, 0, 0)),
            pl.BlockSpec((GM_HEADS, GM_HD), lambda i: (0, 0)),
            pl.BlockSpec((GM_HEADS, GM_HD), lambda i: (0, 0)),
        ],
        out_specs=out_specs,
        out_shape=out_shape,
        scratch_shapes=[pltpu.VMEM((tm, D_MODEL), BF16), pltpu.VMEM((GROUP_W // LANES, tm, LANES), F32)],
        compiler_params=_cparams(("parallel",)),
        name="inproj_prompt",
    )(x, mod, mod, norm_g, w_in_p, w_s, bsb, ln_g, ln_b)


def _inproj_sample_kernel(x_ref, sh_ref, sc_ref, g_ref, w_ref, wd_ref, b0_ref, lng_ref, lnb_ref,
                          q_ref, kv0_ref, kv1_ref, kv2_ref, gm_ref, gvn_ref):
    h = _modulated_norm(x_ref, sh_ref, sc_ref, g_ref).astype(BF16)
    q_ref[...] = _dot(h, w_ref[:, Q_OFF:Q_OFF + ATT_WIDTH])
    for g, kv_ref in enumerate((kv0_ref, kv1_ref, kv2_ref)):
        off = KV_OFF + g * 2 * GROUP_W
        kv_ref[...] = _dot(h, w_ref[:, off:off + 2 * GROUP_W])
    gu = _gelu(_dot(h, w_ref[:, GU_OFF:GU_OFF + GM_WIDTH]))
    gv = _gelu(_dot(h, w_ref[:, GV_OFF:GV_OFF + GM_WIDTH]))
    for hh in range(GM_HEADS):
        sl = slice(hh * GM_HD, (hh + 1) * GM_HD)
        gvn = _gm_layernorm(gv, lng_ref, lnb_ref, hh)
        gvn_ref[:, sl] = gvn
        z = wd_ref[:, sl] * gvn + b0_ref[:, sl]
        gm_ref[:, sl] = (gu[:, sl] * z).astype(BF16)


def _inproj_sample(x, sh, sc, norm_g, w_in_p, wd, b0, ln_g, ln_b):
    n = x.shape[0]
    full = lambda shape: pl.BlockSpec(shape, lambda i: tuple(0 for _ in shape))
    return pl.pallas_call(
        _inproj_sample_kernel,
        grid=(1,),
        in_specs=[
            full((n, D_MODEL)), full((n, D_MODEL)), full((n, D_MODEL)),
            pl.BlockSpec((None, 1, D_MODEL), lambda i: (0, 0, 0)),
            full((D_MODEL, IN_WIDTH)),
            full((1, GM_WIDTH)), full((1, GM_WIDTH)),
            full((GM_HEADS, GM_HD)), full((GM_HEADS, GM_HD)),
        ],
        out_specs=[full((n, ATT_WIDTH)), full((n, 2 * GROUP_W)), full((n, 2 * GROUP_W)),
                   full((n, 2 * GROUP_W)), full((n, GM_WIDTH)), full((n, GM_WIDTH))],
        out_shape=[jax.ShapeDtypeStruct((n, ATT_WIDTH), F32)]
        + [jax.ShapeDtypeStruct((n, 2 * GROUP_W), F32)] * 3
        + [jax.ShapeDtypeStruct((n, GM_WIDTH), BF16), jax.ShapeDtypeStruct((n, GM_WIDTH), F32)],
        compiler_params=_cparams(("arbitrary",)),
        name="inproj_sample",
    )(x, sh, sc, norm_g, w_in_p, wd, b0, ln_g, ln_b)


def _attn_block(qt, kt, vt, off, nkeys):
    lane = lax.broadcasted_iota(jnp.int32, (1, LANES), 1)
    lo = lane < ATT_HD
    qi = lax.broadcasted_iota(jnp.int32, (QBLK, nkeys), 0)
    kk = lax.broadcasted_iota(jnp.int32, (QBLK, nkeys), 1)
    dist = qi - kk + off
    valid = lax.bitcast_convert_type(dist, jnp.uint32) <= jnp.uint32(BAND)
    res = []
    for sel in (lo, jnp.logical_not(lo)):
        qm = jnp.where(sel, qt, jnp.zeros_like(qt))
        s = jnp.where(valid, _dot_nt(qm, kt), NEG_INF)
        m = jnp.max(s, axis=-1, keepdims=True)
        p = jnp.exp(s - m)
        den = jnp.sum(p, axis=-1, keepdims=True)
        o = _dot(p.astype(BF16), vt) * (1.0 / den)
        res.append((o, m + jnp.log(den)))
    o = jnp.where(lo, res[0][0], res[1][0])
    lse = jnp.where(lo, res[0][1], res[1][1])
    return o, lse


def _attn_prompt_kernel(q0_ref, k0_ref, v0_ref, q1_ref, k1_ref, v1_ref, q2_ref, k2_ref, v2_ref,
                        a0_ref, a1_ref, a2_ref, of_scr, lf_scr, to_scr, tl_scr):
    seq = q0_ref.shape[0]

    def run_block(g, qt, kt, vt, off, nkeys, row0):
        o, lse = _attn_block(qt, kt, vt, off, nkeys)
        of_scr[g, pl.ds(row0, QBLK), :] = o
        lf_scr[g, pl.ds(row0, QBLK), :] = lse

    par = 16

    def g0_body(it, carry):
        for u in range(par):
            qb = it * par + u
            qs = pl.multiple_of(qb * QBLK, QBLK)
            ks = pl.multiple_of(jnp.maximum(qb - 1, 0) * QBLK, QBLK)
            run_block(0, q0_ref[pl.ds(qs, QBLK), :], k0_ref[pl.ds(ks, 2 * QBLK), :],
                      v0_ref[pl.ds(ks, 2 * QBLK), :], qs - ks, 2 * QBLK, qs)
        return carry

    lax.fori_loop(0, seq // QBLK // par, g0_body, 0)

    dil1 = DIL_PAIRS[1][1]
    len1 = seq // dil1

    res_per_body = max(par // (len1 // QBLK), 1)

    def g1_body(it, carry):
        for u in range(res_per_body):
            r = it * res_per_body + u
            for qb in range(len1 // QBLK):
                qs = qb * QBLK
                ks = max(qb - 1, 0) * QBLK
                run_block(1, q1_ref[0, r, pl.ds(qs, QBLK), :], k1_ref[0, r, pl.ds(ks, 2 * QBLK), :],
                          v1_ref[0, r, pl.ds(ks, 2 * QBLK), :], qs - ks, 2 * QBLK,
                          pl.multiple_of(r * len1 + qs, QBLK))
        return carry

    lax.fori_loop(0, dil1 // res_per_body, g1_body, 0)

    dil2 = DIL_PAIRS[2][1]
    len2 = seq // dil2

    def g2_body(it, carry):
        for u in range(par):
            r = it * par + u
            pad = it * par + (u ^ 1)
            kt = jnp.concatenate([k2_ref[0, r], k2_ref[0, pad]], axis=0)
            vt = jnp.concatenate([v2_ref[0, r], v2_ref[0, pad]], axis=0)
            run_block(2, q2_ref[0, r], kt, vt, 0, 2 * QBLK, pl.multiple_of(r * len2, QBLK))
        return carry

    lax.fori_loop(0, dil2 // par, g2_body, 0)

    def mix_body(nt, carry):
        t0 = pl.multiple_of(nt * QBLK, QBLK)
        o_nat = [of_scr[0, pl.ds(t0, QBLK), :]]
        l_nat = [lf_scr[0, pl.ds(t0, QBLK), :]]
        for g, dil, length in ((1, dil1, len1), (2, dil2, len2)):
            per = QBLK // dil
            for r in range(dil):
                src = pl.multiple_of(r * length + nt * per, per)
                to_scr[g - 1, pl.ds(r, per, stride=dil), :] = of_scr[g, pl.ds(src, per), :]
                tl_scr[g - 1, pl.ds(r, per, stride=dil), :] = lf_scr[g, pl.ds(src, per), :]
            o_nat.append(to_scr[g - 1])
            l_nat.append(tl_scr[g - 1])
        mx = jnp.maximum(jnp.maximum(l_nat[0], l_nat[1]), l_nat[2])
        e = [jnp.exp(l - mx) for l in l_nat]
        inv = 1.0 / (e[0] + e[1] + e[2])
        for g, a_ref in enumerate((a0_ref, a1_ref, a2_ref)):
            a_ref[pl.ds(t0, QBLK), :] = (o_nat[g] * (e[g] * inv)).astype(BF16)
        return carry

    lax.fori_loop(0, seq // QBLK, mix_body, 0)


def _attn_prompt(folded, batch, seq):
    n = batch * seq
    pairs = GROUP_W // LANES
    in_specs = []
    for _, dil in DIL_PAIRS:
        for _ in range(3):
            if dil == 1:
                in_specs.append(pl.BlockSpec((seq, LANES), lambda b, hp: (b, hp)))
            else:
                in_specs.append(pl.BlockSpec((1, dil, seq // dil, LANES), lambda b, hp: (b, 0, 0, hp)))
    return pl.pallas_call(
        _attn_prompt_kernel,
        grid=(batch, pairs),
        in_specs=in_specs,
        out_specs=[pl.BlockSpec((seq, LANES), lambda b, hp: (b, hp))] * N_DIL,
        out_shape=[jax.ShapeDtypeStruct((n, GROUP_W), BF16)] * N_DIL,
        scratch_shapes=[pltpu.VMEM((N_DIL, seq, LANES), F32), pltpu.VMEM((N_DIL, seq, LANES), F32),
                        pltpu.VMEM((N_DIL - 1, QBLK, LANES), F32), pltpu.VMEM((N_DIL - 1, QBLK, LANES), F32)],
        compiler_params=_cparams(("parallel", "parallel")),
        name="attn_prompt",
    )(*folded)


def _attn_sample_kernel(q_ref, n0_ref, n1_ref, n2_ref, c0_ref, c1_ref, c2_ref, o_ref):
    groups = ((n0_ref, c0_ref), (n1_ref, c1_ref), (n2_ref, c2_ref))

    def head(h, carry):
        outs, lses = [], []
        for g, (n_ref, c_ref) in enumerate(groups):
            dil = DIL_PAIRS[g][1]
            nbuf = c_ref.shape[-1]
            q = q_ref[0, g, h] * (ATT_HD ** -0.5)
            row = lax.broadcasted_iota(jnp.int32, (1, nbuf), 1)
            valid = jnp.bitwise_and(row, dil - 1) == 0
            s_buf = jnp.sum(c_ref[0, 0, h] * q, axis=0, keepdims=True)
            s_buf = jnp.where(valid, s_buf, NEG_INF)
            s_new = jnp.sum(n_ref[0, 0, h] * q, axis=0, keepdims=True)
            m = jnp.maximum(jnp.max(s_buf, axis=1, keepdims=True), s_new)
            p_buf = jnp.exp(s_buf - m)
            p_new = jnp.exp(s_new - m)
            den = jnp.sum(p_buf, axis=1, keepdims=True) + p_new
            o = jnp.sum(c_ref[0, 1, h] * p_buf, axis=1, keepdims=True) + n_ref[0, 1, h] * p_new
            outs.append(o / den)
            lses.append(m + jnp.log(den))
        mx = jnp.maximum(jnp.maximum(lses[0], lses[1]), lses[2])
        e = [jnp.exp(l - mx) for l in lses]
        inv = 1.0 / (e[0] + e[1] + e[2])
        for g in range(N_DIL):
            o_ref[0, g, h] = outs[g] * (e[g] * inv)
        return carry

    lax.fori_loop(0, ATT_HEADS, head, 0)


def _attn_sample(q, new_kvs, caches):
    n_real = q.shape[0]
    assert all(dil & (dil - 1) == 0 and c.shape[-1] == BAND * dil for c, (_, dil) in zip(caches, DIL_PAIRS))
    per_row = lambda a: pl.BlockSpec((1,) + a.shape[1:], lambda b: (b,) + (0,) * (a.ndim - 1))
    return pl.pallas_call(
        _attn_sample_kernel,
        grid=(n_real,),
        in_specs=[per_row(a) for a in (q, *new_kvs, *caches)],
        out_specs=per_row(q),
        out_shape=jax.ShapeDtypeStruct(q.shape, F32),
        compiler_params=_cparams(("arbitrary",)),
        name="attn_sample",
    )(q, *new_kvs, *caches)


SUBLANES = 8


def _oddeven_merge_sort_pairs(n):
    pairs = []
    p = 1
    while p < n:
        k = p
        while k >= 1:
            for j in range(k % p, n - k, 2 * k):
                for i in range(min(k, n - j - k)):
                    if (i + j) // (2 * p) == (i + j + k) // (2 * p):
                        pairs.append((i + j, i + j + k))
            k //= 2
        p *= 2
    return pairs


def _topk_desc(sc, k):
    n = sc.shape[0] // SUBLANES
    rows = [sc[SUBLANES * i:SUBLANES * (i + 1)] for i in range(n)]
    for i, j in _oddeven_merge_sort_pairs(n):
        rows[i], rows[j] = jnp.maximum(rows[i], rows[j]), jnp.minimum(rows[i], rows[j])
    vals = []
    for it in range(k):
        m = jnp.max(rows[0], axis=0, keepdims=True)
        vals.append(m)
        if it + 1 < k:
            win = rows[0] == m
            depth = min(n, k - it)
            for r in range(depth - 1):
                rows[r] = jnp.where(win, rows[r + 1], rows[r])
            rows[depth - 1] = jnp.where(win, NEG_INF, rows[depth - 1])
    return vals


def _outproj_peer_kernel(x_ref, a0_ref, a1_ref, a2_ref, gm_ref, gt1_ref, sh2_ref, sc2_ref, g1_ref, g2_ref,
                         wo_ref, wq_ref, keys_ref,
                         x1_ref, h2t_ref, s1_ref, s2_ref, e2_ref, thr_ref, lse_ref):
    mo = _dot(a0_ref[...], wo_ref[0:GROUP_W, :])
    mo += _dot(a1_ref[...], wo_ref[GROUP_W:2 * GROUP_W, :])
    mo += _dot(a2_ref[...], wo_ref[2 * GROUP_W:3 * GROUP_W, :])
    mo += _dot(gm_ref[...], wo_ref[3 * GROUP_W:4 * GROUP_W, :])
    x1 = x_ref[...] + gt1_ref[...] * _rms(mo, g1_ref[...])
    x1_ref[...] = x1
    h2 = _rms(x1, g2_ref[...]) * (1.0 + sc2_ref[...]) + sh2_ref[...]
    h2t_ref[0] = jnp.transpose(h2).astype(BF16)
    qr = _dot(h2.astype(BF16), wq_ref[...])
    lane_groups = thr_ref.shape[0]
    for h in range(PEER_HEADS):
        tops = []
        for p, s_ref in enumerate((s1_ref, s2_ref)):
            c0 = (h * 2 + p) * PEER_HALF
            sc = _dot_nt(keys_ref[p], qr[:, c0:c0 + PEER_HALF].astype(BF16)) * LOG2E
            for lg in range(lane_groups):
                s_ref[lg, h] = sc[:, lg * LANES:(lg + 1) * LANES]
            tops.append(_topk_desc(sc, PEER_TOPK))
            if p == 1:
                e2 = jnp.exp2(sc - tops[1][0])
                for lg in range(lane_groups):
                    e2_ref[lg, h] = e2[:, lg * LANES:(lg + 1) * LANES]
        t1 = jnp.concatenate(tops[0], axis=0)
        half = PEER_TOPK // 2
        cand = jnp.concatenate(
            [t1 + tops[1][0]]
            + [t1[:half] + tops[1][b] for b in range(1, half)]
            + [jnp.concatenate(tops[1][half:], axis=0) + tops[0][0]], axis=0)
        best = _topk_desc(cand, PEER_TOPK)
        z = jnp.ones_like(best[0])
        for v in best[1:]:
            z = z + jnp.exp2(v - best[0])
        lse = best[0] + jnp.log(z) * LOG2E - tops[1][0]
        for lg in range(lane_groups):
            thr_ref[lg, pl.ds(h, 1), :] = best[-1][:, lg * LANES:(lg + 1) * LANES]
            lse_ref[lg, pl.ds(h, 1), :] = lse[:, lg * LANES:(lg + 1) * LANES]


def _outproj_peer(x, atts, gm, mods, norm_g, w_out, w_q, keys, tm, per_row_mod, tiles_per_seq):
    n = x.shape[0]
    if per_row_mod:
        mod_specs = [pl.BlockSpec((tm, D_MODEL), lambda i: (i, 0))] * 3
    else:
        mod_specs = [pl.BlockSpec((None, 1, D_MODEL), lambda i, k=k: ((i // tiles_per_seq) * 6 + k, 0, 0))
                     for k in (2, 3, 4)]
    row_spec = lambda w: pl.BlockSpec((tm, w), lambda i: (i, 0))
    const = lambda shape: pl.BlockSpec(shape, lambda i: tuple(0 for _ in shape))
    return pl.pallas_call(
        _outproj_peer_kernel,
        grid=(n // tm,),
        in_specs=[row_spec(D_MODEL)] + [row_spec(GROUP_W)] * 4 + mod_specs
        + [pl.BlockSpec((None, 1, D_MODEL), lambda i: (1, 0, 0)),
           pl.BlockSpec((None, 1, D_MODEL), lambda i: (2, 0, 0)),
           const((D_MODEL, D_MODEL)), const((D_MODEL, D_MODEL)),
           const((2, N_KEYS, PEER_HALF))],
        out_specs=[row_spec(D_MODEL),
                   pl.BlockSpec((1, D_MODEL, tm), lambda i: (i, 0, 0)),
                   pl.BlockSpec((tm // LANES, PEER_HEADS, N_KEYS, LANES), lambda i: (i, 0, 0, 0)),
                   pl.BlockSpec((tm // LANES, PEER_HEADS, N_KEYS, LANES), lambda i: (i, 0, 0, 0)),
                   pl.BlockSpec((tm // LANES, PEER_HEADS, N_KEYS, LANES), lambda i: (i, 0, 0, 0)),
                   pl.BlockSpec((tm // LANES, PEER_HEADS, LANES), lambda i: (i, 0, 0)),
                   pl.BlockSpec((tm // LANES, PEER_HEADS, LANES), lambda i: (i, 0, 0))],
        out_shape=[jax.ShapeDtypeStruct((n, D_MODEL), F32),
                   jax.ShapeDtypeStruct((n // tm, D_MODEL, tm), BF16),
                   jax.ShapeDtypeStruct((n // LANES, PEER_HEADS, N_KEYS, LANES), F32),
                   jax.ShapeDtypeStruct((n // LANES, PEER_HEADS, N_KEYS, LANES), F32),
                   jax.ShapeDtypeStruct((n // LANES, PEER_HEADS, N_KEYS, LANES), F32),
                   jax.ShapeDtypeStruct((n // LANES, PEER_HEADS, LANES), F32),
                   jax.ShapeDtypeStruct((n // LANES, PEER_HEADS, LANES), F32)],
        compiler_params=_cparams(("parallel",)),
        name="outproj_peer",
    )(x, *atts, gm, *mods, norm_g, norm_g, w_out, w_q, keys)


def _expert_kernel(h2t_ref, u_ref, vt_ref, s1_ref, s2_ref, e2_ref, thr_ref, lse_ref, x1_ref, gt2_ref, g3_ref,
                   y_ref, acc_scr, g_scr, w_scr, rowb_scr, thrb_scr, tile_scr):
    j = pl.program_id(1)
    n_tiles = pl.num_programs(1) - 1
    te = u_ref.shape[0]
    rows_per_tile = te // N_KEYS
    lane_groups = s1_ref.shape[0]
    n_chunks, _, chunk_w = h2t_ref.shape
    lg_per_chunk = chunk_w // LANES
    n_rows = rows_per_tile * PEER_HEADS
    ch = 8

    @pl.when(j == 0)
    def _():
        acc_scr[...] = jnp.zeros_like(acc_scr)
        g_scr[...] = jnp.zeros_like(g_scr)
        for lg in range(lane_groups):
            for h in range(PEER_HEADS):
                thrb_scr[lg, h] = jnp.broadcast_to(thr_ref[lg, pl.ds(h, 1), :], (ch, LANES))

    jt = jnp.minimum(j, n_tiles - 1)
    for lg in range(lane_groups):
        for rr in range(rows_per_tile):
            tile = tile_scr.at[lg * rows_per_tile + rr]
            for h in range(PEER_HEADS):
                tile[pl.ds(h, 1), :] = s1_ref[lg, h, pl.ds(jt * rows_per_tile + rr, 1), :]
            rows8 = tile[...]
            e8 = jnp.exp2(rows8 - lse_ref[lg])
            for h in range(PEER_HEADS):
                rowb_scr[lg, rr * PEER_HEADS + h] = jnp.broadcast_to(rows8[h:h + 1], (ch, LANES))
                rowb_scr[lg, n_rows + rr * PEER_HEADS + h] = jnp.broadcast_to(e8[h:h + 1], (ch, LANES))

    def build_gates(lg, part=0, n_parts=1):
        per_part = N_KEYS // ch // n_parts
        for ci in range(per_part):
            cc = part * per_part + ci
            c0 = cc * ch if n_parts == 1 else pl.multiple_of(cc * ch, ch)
            gates = [None] * rows_per_tile
            for h in range(PEER_HEADS):
                b = s2_ref[lg, h, pl.ds(c0, ch), :]
                eb = e2_ref[lg, h, pl.ds(c0, ch), :]
                t = thrb_scr[lg, h]
                for rr in range(rows_per_tile):
                    a = rowb_scr[lg, rr * PEER_HEADS + h]
                    ea = rowb_scr[lg, n_rows + rr * PEER_HEADS + h]
                    term = jnp.where(a + b >= t, ea * eb, 0.0)
                    gates[rr] = term if gates[rr] is None else gates[rr] + term
            for rr in range(rows_per_tile):
                g_scr[lg, pl.ds(rr * N_KEYS + c0, ch), :] = gates[rr]

    for k in range(n_chunks):
        act = _dot(u_ref[...], h2t_ref[k])
        for q in range(lg_per_chunk):
            lg = k * lg_per_chunk + q
            w_scr[:, lg * LANES:(lg + 1) * LANES] = (
                g_scr[lg] * _gelu_sigmoid(act[:, q * LANES:(q + 1) * LANES])).astype(BF16)

    parts = 2 if lane_groups == 1 else 1
    d_slab = acc_scr.shape[0] // (lane_groups * parts)

    def piece(p, carry):
        d0 = pl.multiple_of(p * d_slab, d_slab)
        acc_scr[pl.ds(d0, d_slab), :] += _dot(vt_ref[pl.ds(d0, d_slab), :], w_scr[...])
        if parts == 1:
            build_gates(p)
        else:
            build_gates(0, p, parts)
        return carry

    lax.fori_loop(0, lane_groups * parts, piece, 0)

    @pl.when(j == n_tiles)
    def _():
        peer = jnp.transpose(acc_scr[...])
        y_ref[...] = x1_ref[...] + gt2_ref[...] * _rms(peer, g3_ref[...])


def _experts(h2t, u_bf, vt_bf, s1, s2, e2, thr, lse, x1, gt2, norm_g, tm, per_row_mod, tiles_per_seq):
    chunk_w = h2t.shape[2]
    n = h2t.shape[0] * chunk_w
    te = 512
    n_tiles = N_EXPERTS // te
    prev_tile = lambda j: jnp.maximum(j - 1, 0)
    if per_row_mod:
        gt2_spec = pl.BlockSpec((tm, D_MODEL), lambda i, j: (i, 0))
    else:
        gt2_spec = pl.BlockSpec((None, 1, D_MODEL), lambda i, j: ((i // tiles_per_seq) * 6 + 5, 0, 0))
    return pl.pallas_call(
        _expert_kernel,
        grid=(n // tm, n_tiles + 1),
        in_specs=[
            pl.BlockSpec((tm // chunk_w, D_MODEL, chunk_w), lambda i, j: (i, 0, 0)),
            pl.BlockSpec((te, D_MODEL), lambda i, j: (prev_tile(j), 0)),
            pl.BlockSpec((D_MODEL, te), lambda i, j: (0, prev_tile(j))),
            pl.BlockSpec((tm // LANES, PEER_HEADS, N_KEYS, LANES), lambda i, j: (i, 0, 0, 0)),
            pl.BlockSpec((tm // LANES, PEER_HEADS, N_KEYS, LANES), lambda i, j: (i, 0, 0, 0)),
            pl.BlockSpec((tm // LANES, PEER_HEADS, N_KEYS, LANES), lambda i, j: (i, 0, 0, 0)),
            pl.BlockSpec((tm // LANES, PEER_HEADS, LANES), lambda i, j: (i, 0, 0)),
            pl.BlockSpec((tm // LANES, PEER_HEADS, LANES), lambda i, j: (i, 0, 0)),
            pl.BlockSpec((tm, D_MODEL), lambda i, j: (i, 0)),
            gt2_spec,
            pl.BlockSpec((None, 1, D_MODEL), lambda i, j: (3, 0, 0)),
        ],
        out_specs=pl.BlockSpec((tm, D_MODEL), lambda i, j: (i, 0)),
        out_shape=jax.ShapeDtypeStruct((n, D_MODEL), F32),
        scratch_shapes=[pltpu.VMEM((D_MODEL, tm), F32), pltpu.VMEM((tm // LANES, te, LANES), F32),
                        pltpu.VMEM((te, tm), BF16),
                        pltpu.VMEM((tm // LANES, 2 * (te // N_KEYS) * PEER_HEADS, 8, LANES), F32),
                        pltpu.VMEM((tm // LANES, PEER_HEADS, 8, LANES), F32),
                        pltpu.VMEM((tm // LANES * (te // N_KEYS), PEER_HEADS, LANES), F32)],
        compiler_params=_cparams(("parallel", "arbitrary")),
        name="peer_experts",
    )(h2t, u_bf, vt_bf, s1, s2, e2, thr, lse, x1, gt2, norm_g)


def _cast_transpose_kernel(v_ref, o_ref):
    o_ref[...] = jnp.transpose(v_ref[...]).astype(BF16)


def _cast_transpose(v):
    rows, cols = v.shape
    tr = 1024
    return pl.pallas_call(
        _cast_transpose_kernel,
        grid=(rows // tr,),
        in_specs=[pl.BlockSpec((tr, cols), lambda i: (i, 0))],
        out_specs=pl.BlockSpec((cols, tr), lambda i: (0, i)),
        out_shape=jax.ShapeDtypeStruct((cols, rows), BF16),
        compiler_params=_cparams(("parallel",)),
        name="cast_transpose",
    )(v)


def _permute_w_in(w_in):
    q = w_in[:, :ATT_WIDTH]
    k = w_in[:, ATT_WIDTH:2 * ATT_WIDTH]
    v = w_in[:, 2 * ATT_WIDTH:3 * ATT_WIDTH]
    parts = [q]
    for g in range(N_DIL):
        parts += [k[:, g * GROUP_W:(g + 1) * GROUP_W], v[:, g * GROUP_W:(g + 1) * GROUP_W]]
    parts.append(w_in[:, 3 * ATT_WIDTH:])
    return jnp.concatenate(parts, axis=1).astype(BF16)


def kernel(x_prompt, x_sample, c_prompt, c_sample, cache_kv_dil1, cache_kv_dil4, cache_kv_dil16, w_ada, b_ada, norm_g, w_in, gm_ln_g, gm_ln_b, w_s, b_s, w_out, w_q, sub_keys, expert_u, expert_v):
    batch, seq, _ = x_prompt.shape
    n_s = x_sample.shape[0] * x_sample.shape[1]
    assert w_ada.shape[0] == 1 and x_sample.shape[1] == 1
    n_p = batch * seq
    pad_s = 128

    w_in_p = _permute_w_in(w_in[0])
    w_out_b = w_out[0].astype(BF16)
    w_q_b = w_q[0].astype(BF16)
    keys_b = sub_keys[0].astype(BF16)
    u_b = expert_u[0].astype(BF16)
    vt_b = _cast_transpose(expert_v[0])
    ng = norm_g[0].reshape(4, 1, D_MODEL)
    ln_g, ln_b = gm_ln_g[0], gm_ln_b[0]

    rows = batch + n_s
    c_all = jnp.zeros((16, D_MODEL), F32).at[:rows].set(jnp.concatenate([c_prompt, c_sample], axis=0))
    mod = _ada(c_all, w_ada[0], b_ada)
    mod_p = mod.reshape(16 * 6, 1, D_MODEL)
    mod_s = jnp.zeros((pad_s, 6, D_MODEL), F32).at[:n_s].set(mod[batch:rows].reshape(n_s, 6, D_MODEL))
    mods_s = [mod_s[:, k] for k in range(6)]

    xp = x_prompt.reshape(n_p, D_MODEL)
    bsb = jnp.broadcast_to(b_s[0][:, :, None], (GM_HEADS, CHUNK, GM_HD))
    outs = _inproj_prompt(xp, mod_p, ng, w_in_p, w_s[0], bsb, ln_g, ln_b, batch, seq)
    folded, kv_p, gm_p = outs[:9], outs[9:12], outs[12]
    att_p = _attn_prompt(folded, batch, seq)
    tm_p = 256
    x1_p, h2t_p, *sel_p = _outproj_peer(
        xp, att_p, gm_p, [mod_p] * 3, ng, w_out_b, w_q_b, keys_b, tm_p, False, seq // tm_p)
    tm_e = 512
    y_p = _experts(h2t_p, u_b, vt_b, *sel_p, x1_p, mod_p, ng, tm_e, False, seq // tm_e)

    xs = jnp.zeros((pad_s, D_MODEL), F32).at[:n_s].set(x_sample.reshape(n_s, D_MODEL))
    wd = jnp.repeat(w_s[0][:, 0, 0], GM_HD)[None, :]
    b0 = jnp.repeat(b_s[0][:, 0], GM_HD)[None, :]
    q_s, kv0_s, kv1_s, kv2_s, gm_s, gvn_s = _inproj_sample(
        xs, mods_s[0], mods_s[1], ng, w_in_p, wd, b0, ln_g, ln_b)
    new_kv_s = [kv[:n_s].reshape(n_s, 2, ATT_HEADS, ATT_HD) for kv in (kv0_s, kv1_s, kv2_s)]
    caches_t = [jnp.transpose(c[0], (0, 2, 3, 4, 1)) for c in (cache_kv_dil1, cache_kv_dil4, cache_kv_dil16)]
    att_s = _attn_sample(q_s[:n_s].reshape(n_s, N_DIL, ATT_HEADS, ATT_HD, 1),
                         [kv[..., None] for kv in new_kv_s], caches_t)
    att_s = att_s.reshape(n_s, N_DIL, GROUP_W).astype(BF16)
    att_s = [jnp.zeros((pad_s, GROUP_W), BF16).at[:n_s].set(att_s[:, g]) for g in range(N_DIL)]
    x1_s, h2t_s, *sel_s = _outproj_peer(
        xs, att_s, gm_s, mods_s[2:5], ng, w_out_b, w_q_b, keys_b, pad_s, True, 1)
    y_s = _experts(h2t_s, u_b, vt_b, *sel_s, x1_s, mods_s[5], ng, pad_s, True, 1)

    y_prompt = y_p.reshape(batch, seq, D_MODEL)
    y_sample = y_s[:n_s].reshape(x_sample.shape)
    new_kv_p = []
    for kv_t, (win, _) in zip(kv_p, DIL_PAIRS):
        keep = min(win, seq)
        kv_t = kv_t[:, :, seq - keep:].reshape(batch, 2, ATT_HEADS, ATT_HD, keep)
        new_kv_p.append(jnp.transpose(kv_t, (0, 4, 1, 2, 3))[None])
    new_kv_s = [kv.reshape(1, n_s, 1, 2, ATT_HEADS, ATT_HD) for kv in new_kv_s]
    state_gv = gvn_s[:n_s].reshape(1, n_s, 1, GM_HEADS, GM_HD)
    return (y_prompt, y_sample, *new_kv_p, *new_kv_s, state_gv)
```

```python
import functools
import math

import jax
import jax.numpy as jnp
from jax import lax
from jax.experimental import pallas as pl
from jax.experimental.pallas import tpu as pltpu

F32 = jnp.float32
BF16 = jnp.bfloat16

D_MODEL = 2048
ATT_HD = 64
ATT_HEADS = 8
GROUP_W = ATT_HEADS * ATT_HD
DIL_PAIRS = ((128, 1), (512, 4), (2048, 16))
N_DIL = 3
ATT_WIDTH = N_DIL * GROUP_W
QBLK = 128
BAND = 128
CHUNK = 128
GM_HEADS = 4
GM_HD = 128
GM_WIDTH = GM_HEADS * GM_HD
IN_WIDTH = 3 * ATT_WIDTH + 2 * GM_WIDTH
N_KEYS = 128
N_EXPERTS = N_KEYS * N_KEYS
PEER_HEADS = 8
PEER_HALF = 128
PEER_TOPK = 16
EPS = 1e-6
NEG_INF = float("-inf")
LOG2E = 1.4426950408889634

LANES = 128
VMEM_LIMIT = 56 * 1024 * 1024

Q_OFF = 0
KV_OFF = ATT_WIDTH
GU_OFF = ATT_WIDTH + N_DIL * 2 * GROUP_W
GV_OFF = GU_OFF + GM_WIDTH


def _cparams(sem, flags=None):
    return pltpu.CompilerParams(dimension_semantics=sem, vmem_limit_bytes=VMEM_LIMIT, flags=flags)


def _gelu(x):
    c = math.sqrt(2.0 / math.pi)
    return 0.5 * x * (1.0 + jnp.tanh(c * (x + 0.044715 * (x * x * x))))


def _gelu_sigmoid(x):
    k2 = -2.0 * math.sqrt(2.0 / math.pi) * LOG2E
    k1 = k2 * 0.044715
    return x / (1.0 + jnp.exp2(x * (k2 + k1 * (x * x))))


def _rms(x, g):
    return x * lax.rsqrt(jnp.mean(x * x, axis=-1, keepdims=True) + EPS) * g


def _dot(a, b):
    return jnp.dot(a, b, preferred_element_type=F32)


def _dot_nt(a, b):
    return lax.dot_general(a, b, (((1,), (1,)), ((), ())), preferred_element_type=F32)


def _ada_kernel(c_ref, w_ref, b_ref, o_ref):
    c = c_ref[...]
    act = (c / (1.0 + jnp.exp(-c))).astype(BF16)
    o_ref[...] = _dot(act, w_ref[...].astype(BF16)) + b_ref[...]


def _ada(c_all, w_ada, b_ada):
    rows = c_all.shape[0]
    n_out = w_ada.shape[1]
    tn = 1024
    return pl.pallas_call(
        _ada_kernel,
        grid=(n_out // tn,),
        in_specs=[
            pl.BlockSpec((rows, D_MODEL), lambda j: (0, 0)),
            pl.BlockSpec((D_MODEL, tn), lambda j: (0, j)),
            pl.BlockSpec((1, tn), lambda j: (0, j)),
        ],
        out_specs=pl.BlockSpec((rows, tn), lambda j: (0, j)),
        out_shape=jax.ShapeDtypeStruct((rows, n_out), F32),
        compiler_params=_cparams(("parallel",)),
        name="ada_mod",
    )(c_all, w_ada, b_ada)


def _modulated_norm(x_ref, sh_ref, sc_ref, g_ref):
    x = x_ref[...]
    return _rms(x, g_ref[...]) * (1.0 + sc_ref[...]) + sh_ref[...]


def _gm_layernorm(gv, lng_ref, lnb_ref, hh):
    v = gv[:, hh * GM_HD:(hh + 1) * GM_HD]
    mu = jnp.mean(v, axis=-1, keepdims=True)
    var = jnp.mean(jnp.square(v - mu), axis=-1, keepdims=True)
    return (v - mu) * lax.rsqrt(var + EPS) * lng_ref[pl.ds(hh, 1), :] + lnb_ref[pl.ds(hh, 1), :]


def _inproj_prompt_kernel(x_ref, sh_ref, sc_ref, g_ref, w_ref, ws_ref, bsb_ref, lng_ref, lnb_ref,
                          q0_ref, k0_ref, v0_ref, q1_ref, k1_ref, v1_ref, q2_ref, k2_ref, v2_ref,
                          kv0_ref, kv1_ref, kv2_ref, gm_ref, h_scr, y_scr):
    tm = x_ref.shape[0]
    h_scr[...] = _modulated_norm(x_ref, sh_ref, sc_ref, g_ref).astype(BF16)

    def proj(off):
        return _dot(h_scr[...], w_ref[:, off:off + GROUP_W])

    folded = ((q0_ref, k0_ref, v0_ref), (q1_ref, k1_ref, v1_ref), (q2_ref, k2_ref, v2_ref))
    kv_refs = (kv0_ref, kv1_ref, kv2_ref)
    for g, (_, dil) in enumerate(DIL_PAIRS):
        offs = (Q_OFF + g * GROUP_W, KV_OFF + g * 2 * GROUP_W, KV_OFF + g * 2 * GROUP_W + GROUP_W)
        for which, off in enumerate(offs):
            y = proj(off)
            if which > 0:
                kv_refs[g][0, (which - 1) * GROUP_W:which * GROUP_W, :] = jnp.transpose(y)
            else:
                y = y * (ATT_HD ** -0.5)
            dst = folded[g][which]
            if dil == 1:
                dst[...] = y.astype(BF16)
            else:
                for cb in range(GROUP_W // LANES):
                    y_scr[cb] = y[:, cb * LANES:(cb + 1) * LANES]
                for r in range(dil):
                    for cb in range(GROUP_W // LANES):
                        dst[0, r, :, cb * LANES:(cb + 1) * LANES] = (
                            y_scr[cb, pl.ds(r, tm // dil, stride=dil), :].astype(BF16))

    gu = _gelu(proj(GU_OFF))
    gv = _gelu(proj(GV_OFF))
    row = lax.broadcasted_iota(jnp.int32, (CHUNK, CHUNK), 0)
    col = lax.broadcasted_iota(jnp.int32, (CHUNK, CHUNK), 1)
    for hh in range(GM_HEADS):
        gvn = _gm_layernorm(gv, lng_ref, lnb_ref, hh).astype(BF16)
        w_tril = jnp.where(row >= col, ws_ref[hh], 0.0).astype(BF16)
        for ci in range(tm // CHUNK):
            z = _dot(w_tril, gvn[ci * CHUNK:(ci + 1) * CHUNK]) + bsb_ref[hh]
            gu_blk = gu[ci * CHUNK:(ci + 1) * CHUNK, hh * GM_HD:(hh + 1) * GM_HD]
            gm_ref[ci * CHUNK:(ci + 1) * CHUNK, hh * GM_HD:(hh + 1) * GM_HD] = (gu_blk * z).astype(BF16)


def _inproj_prompt(x, mod, norm_g, w_in_p, w_s, bsb, ln_g, ln_b, batch, seq):
    n = x.shape[0]
    tm = 256
    tiles_per_seq = seq // tm

    def mod_spec(k):
        return pl.BlockSpec((None, 1, D_MODEL), lambda i: ((i // tiles_per_seq) * 6 + k, 0, 0))

    def folded_spec(dil):
        if dil == 1:
            return pl.BlockSpec((tm, GROUP_W), lambda i: (i, 0))
        return pl.BlockSpec((1, dil, tm // dil, GROUP_W),
                            lambda i: (i // tiles_per_seq, 0, i % tiles_per_seq, 0))

    def folded_shape(dil):
        if dil == 1:
            return jax.ShapeDtypeStruct((n, GROUP_W), BF16)
        return jax.ShapeDtypeStruct((batch, dil, seq // dil, GROUP_W), BF16)

    out_specs, out_shape = [], []
    for _, dil in DIL_PAIRS:
        for _ in range(3):
            out_specs.append(folded_spec(dil))
            out_shape.append(folded_shape(dil))
    for _ in range(N_DIL):
        out_specs.append(pl.BlockSpec((1, 2 * GROUP_W, tm),
                                      lambda i: (i // tiles_per_seq, 0, i % tiles_per_seq)))
        out_shape.append(jax.ShapeDtypeStruct((batch, 2 * GROUP_W, seq), F32))
    out_specs.append(pl.BlockSpec((tm, GM_WIDTH), lambda i: (i, 0)))
    out_shape.append(jax.ShapeDtypeStruct((n, GM_WIDTH), BF16))

    return pl.pallas_call(
        _inproj_prompt_kernel,
        grid=(n // tm,),
        in_specs=[
            pl.BlockSpec((tm, D_MODEL), lambda i: (i, 0)),
            mod_spec(0), mod_spec(1),
            pl.BlockSpec((None, 1, D_MODEL), lambda i: (0, 0, 0)),
            pl.BlockSpec((D_MODEL, IN_WIDTH), lambda i: (0, 0)),
            pl.BlockSpec((GM_HEADS, CHUNK, CHUNK), lambda i: (0, 0, 0)),
            pl.BlockSpec((GM_HEADS, CHUNK, GM_HD), lambda i: (0, 0, 0)),
            pl.BlockSpec((GM_HEADS, GM_HD), lambda i: (0, 0)),
            pl.BlockSpec((GM_HEADS, GM_HD), lambda i: (0, 0)),
        ],
        out_specs=out_specs,
        out_shape=out_shape,
        scratch_shapes=[pltpu.VMEM((tm, D_MODEL), BF16), pltpu.VMEM((GROUP_W // LANES, tm, LANES), F32)],
        compiler_params=_cparams(("parallel",)),
        name="inproj_prompt",
    )(x, mod, mod, norm_g, w_in_p, w_s, bsb, ln_g, ln_b)


def _inproj_sample_kernel(x_ref, sh_ref, sc_ref, g_ref, w_ref, wd_ref, b0_ref, lng_ref, lnb_ref,
                          q_ref, kv0_ref, kv1_ref, kv2_ref, gm_ref, gvn_ref):
    h = _modulated_norm(x_ref, sh_ref, sc_ref, g_ref).astype(BF16)
    q_ref[...] = _dot(h, w_ref[:, Q_OFF:Q_OFF + ATT_WIDTH])
    for g, kv_ref in enumerate((kv0_ref, kv1_ref, kv2_ref)):
        off = KV_OFF + g * 2 * GROUP_W
        kv_ref[...] = _dot(h, w_ref[:, off:off + 2 * GROUP_W])
    gu = _gelu(_dot(h, w_ref[:, GU_OFF:GU_OFF + GM_WIDTH]))
    gv = _gelu(_dot(h, w_ref[:, GV_OFF:GV_OFF + GM_WIDTH]))
    for hh in range(GM_HEADS):
        sl = slice(hh * GM_HD, (hh + 1) * GM_HD)
        gvn = _gm_layernorm(gv, lng_ref, lnb_ref, hh)
        gvn_ref[:, sl] = gvn
        z = wd_ref[:, sl] * gvn + b0_ref[:, sl]
        gm_ref[:, sl] = (gu[:, sl] * z).astype(BF16)


def _inproj_sample(x, sh, sc, norm_g, w_in_p, wd, b0, ln_g, ln_b):
    n = x.shape[0]
    full = lambda shape: pl.BlockSpec(shape, lambda i: tuple(0 for _ in shape))
    return pl.pallas_call(
        _inproj_sample_kernel,
        grid=(1,),
        in_specs=[
            full((n, D_MODEL)), full((n, D_MODEL)), full((n, D_MODEL)),
            pl.BlockSpec((None, 1, D_MODEL), lambda i: (0, 0, 0)),
            full((D_MODEL, IN_WIDTH)),
            full((1, GM_WIDTH)), full((1, GM_WIDTH)),
            full((GM_HEADS, GM_HD)), full((GM_HEADS, GM_HD)),
        ],
        out_specs=[full((n, ATT_WIDTH)), full((n, 2 * GROUP_W)), full((n, 2 * GROUP_W)),
                   full((n, 2 * GROUP_W)), full((n, GM_WIDTH)), full((n, GM_WIDTH))],
        out_shape=[jax.ShapeDtypeStruct((n, ATT_WIDTH), F32)]
        + [jax.ShapeDtypeStruct((n, 2 * GROUP_W), F32)] * 3
        + [jax.ShapeDtypeStruct((n, GM_WIDTH), BF16), jax.ShapeDtypeStruct((n, GM_WIDTH), F32)],
        compiler_params=_cparams(("arbitrary",)),
        name="inproj_sample",
    )(x, sh, sc, norm_g, w_in_p, wd, b0, ln_g, ln_b)


def _attn_block(qt, kt, vt, off, nkeys):
    lane = lax.broadcasted_iota(jnp.int32, (1, LANES), 1)
    lo = lane < ATT_HD
    qi = lax.broadcasted_iota(jnp.int32, (QBLK, nkeys), 0)
    kk = lax.broadcasted_iota(jnp.int32, (QBLK, nkeys), 1)
    dist = qi - kk + off
    valid = lax.bitcast_convert_type(dist, jnp.uint32) <= jnp.uint32(BAND)
    res = []
    for sel in (lo, jnp.logical_not(lo)):
        qm = jnp.where(sel, qt, jnp.zeros_like(qt))
        s = jnp.where(valid, _dot_nt(qm, kt), NEG_INF)
        m = jnp.max(s, axis=-1, keepdims=True)
        p = jnp.exp(s - m)
        den = jnp.sum(p, axis=-1, keepdims=True)
        o = _dot(p.astype(BF16), vt) * (1.0 / den)
        res.append((o, m + jnp.log(den)))
    o = jnp.where(lo, res[0][0], res[1][0])
    lse = jnp.where(lo, res[0][1], res[1][1])
    return o, lse


def _attn_prompt_kernel(q0_ref, k0_ref, v0_ref, q1_ref, k1_ref, v1_ref, q2_ref, k2_ref, v2_ref,
                        a0_ref, a1_ref, a2_ref, of_scr, lf_scr, to_scr, tl_scr):
    seq = q0_ref.shape[0]

    def run_block(g, qt, kt, vt, off, nkeys, row0):
        o, lse = _attn_block(qt, kt, vt, off, nkeys)
        of_scr[g, pl.ds(row0, QBLK), :] = o
        lf_scr[g, pl.ds(row0, QBLK), :] = lse

    par = 16

    def g0_body(it, carry):
        for u in range(par):
            qb = it * par + u
            qs = pl.multiple_of(qb * QBLK, QBLK)
            ks = pl.multiple_of(jnp.maximum(qb - 1, 0) * QBLK, QBLK)
            run_block(0, q0_ref[pl.ds(qs, QBLK), :], k0_ref[pl.ds(ks, 2 * QBLK), :],
                      v0_ref[pl.ds(ks, 2 * QBLK), :], qs - ks, 2 * QBLK, qs)
        return carry

    lax.fori_loop(0, seq // QBLK // par, g0_body, 0)

    dil1 = DIL_PAIRS[1][1]
    len1 = seq // dil1

    res_per_body = max(par // (len1 // QBLK), 1)

    def g1_body(it, carry):
        for u in range(res_per_body):
            r = it * res_per_body + u
            for qb in range(len1 // QBLK):
                qs = qb * QBLK
                ks = max(qb - 1, 0) * QBLK
                run_block(1, q1_ref[0, r, pl.ds(qs, QBLK), :], k1_ref[0, r, pl.ds(ks, 2 * QBLK), :],
                          v1_ref[0, r, pl.ds(ks, 2 * QBLK), :], qs - ks, 2 * QBLK,
                          pl.multiple_of(r * len1 + qs, QBLK))
        return carry

    lax.fori_loop(0, dil1 // res_per_body, g1_body, 0)

    dil2 = DIL_PAIRS[2][1]
    len2 = seq // dil2

    def g2_body(it, carry):
        for u in range(par):
            r = it * par + u
            pad = it * par + (u ^ 1)
            kt = jnp.concatenate([k2_ref[0, r], k2_ref[0, pad]], axis=0)
            vt = jnp.concatenate([v2_ref[0, r], v2_ref[0, pad]], axis=0)
            run_block(2, q2_ref[0, r], kt, vt, 0, 2 * QBLK, pl.multiple_of(r * len2, QBLK))
        return carry

    lax.fori_loop(0, dil2 // par, g2_body, 0)

    def mix_body(nt, carry):
        t0 = pl.multiple_of(nt * QBLK, QBLK)
        o_nat = [of_scr[0, pl.ds(t0, QBLK), :]]
        l_nat = [lf_scr[0, pl.ds(t0, QBLK), :]]
        for g, dil, length in ((1, dil1, len1), (2, dil2, len2)):
            per = QBLK // dil
            for r in range(dil):
                src = pl.multiple_of(r * length + nt * per, per)
                to_scr[g - 1, pl.ds(r, per, stride=dil), :] = of_scr[g, pl.ds(src, per), :]
                tl_scr[g - 1, pl.ds(r, per, stride=dil), :] = lf_scr[g, pl.ds(src, per), :]
            o_nat.append(to_scr[g - 1])
            l_nat.append(tl_scr[g - 1])
        mx = jnp.maximum(jnp.maximum(l_nat[0], l_nat[1]), l_nat[2])
        e = [jnp.exp(l - mx) for l in l_nat]
        inv = 1.0 / (e[0] + e[1] + e[2])
        for g, a_ref in enumerate((a0_ref, a1_ref, a2_ref)):
            a_ref[pl.ds(t0, QBLK), :] = (o_nat[g] * (e[g] * inv)).astype(BF16)
        return carry

    lax.fori_loop(0, seq // QBLK, mix_body, 0)


def _attn_prompt(folded, batch, seq):
    n = batch * seq
    pairs = GROUP_W // LANES
    in_specs = []
    for _, dil in DIL_PAIRS:
        for _ in range(3):
            if dil == 1:
                in_specs.append(pl.BlockSpec((seq, LANES), lambda b, hp: (b, hp)))
            else:
                in_specs.append(pl.BlockSpec((1, dil, seq // dil, LANES), lambda b, hp: (b, 0, 0, hp)))
    return pl.pallas_call(
        _attn_prompt_kernel,
        grid=(batch, pairs),
        in_specs=in_specs,
        out_specs=[pl.BlockSpec((seq, LANES), lambda b, hp: (b, hp))] * N_DIL,
        out_shape=[jax.ShapeDtypeStruct((n, GROUP_W), BF16)] * N_DIL,
        scratch_shapes=[pltpu.VMEM((N_DIL, seq, LANES), F32), pltpu.VMEM((N_DIL, seq, LANES), F32),
                        pltpu.VMEM((N_DIL - 1, QBLK, LANES), F32), pltpu.VMEM((N_DIL - 1, QBLK, LANES), F32)],
        compiler_params=_cparams(("parallel", "parallel")),
        name="attn_prompt",
    )(*folded)


def _attn_sample_kernel(q_ref, n0_ref, n1_ref, n2_ref, c0_ref, c1_ref, c2_ref, o_ref):
    groups = ((n0_ref, c0_ref), (n1_ref, c1_ref), (n2_ref, c2_ref))

    def head(h, carry):
        outs, lses = [], []
        for g, (n_ref, c_ref) in enumerate(groups):
            dil = DIL_PAIRS[g][1]
            nbuf = c_ref.shape[-1]
            q = q_ref[0, g, h] * (ATT_HD ** -0.5)
            row = lax.broadcasted_iota(jnp.int32, (1, nbuf), 1)
            valid = jnp.bitwise_and(row, dil - 1) == 0
            s_buf = jnp.sum(c_ref[0, 0, h] * q, axis=0, keepdims=True)
            s_buf = jnp.where(valid, s_buf, NEG_INF)
            s_new = jnp.sum(n_ref[0, 0, h] * q, axis=0, keepdims=True)
            m = jnp.maximum(jnp.max(s_buf, axis=1, keepdims=True), s_new)
            p_buf = jnp.exp(s_buf - m)
            p_new = jnp.exp(s_new - m)
            den = jnp.sum(p_buf, axis=1, keepdims=True) + p_new
            o = jnp.sum(c_ref[0, 1, h] * p_buf, axis=1, keepdims=True) + n_ref[0, 1, h] * p_new
            outs.append(o / den)
            lses.append(m + jnp.log(den))
        mx = jnp.maximum(jnp.maximum(lses[0], lses[1]), lses[2])
        e = [jnp.exp(l - mx) for l in lses]
        inv = 1.0 / (e[0] + e[1] + e[2])
        for g in range(N_DIL):
            o_ref[0, g, h] = outs[g] * (e[g] * inv)
        return carry

    lax.fori_loop(0, ATT_HEADS, head, 0)


def _attn_sample(q, new_kvs, caches):
    n_real = q.shape[0]
    assert all(dil & (dil - 1) == 0 and c.shape[-1] == BAND * dil for c, (_, dil) in zip(caches, DIL_PAIRS))
    per_row = lambda a: pl.BlockSpec((1,) + a.shape[1:], lambda b: (b,) + (0,) * (a.ndim - 1))
    return pl.pallas_call(
        _attn_sample_kernel,
        grid=(n_real,),
        in_specs=[per_row(a) for a in (q, *new_kvs, *caches)],
        out_specs=per_row(q),
        out_shape=jax.ShapeDtypeStruct(q.shape, F32),
        compiler_params=_cparams(("arbitrary",)),
        name="attn_sample",
    )(q, *new_kvs, *caches)


SUBLANES = 8


def _oddeven_merge_sort_pairs(n):
    pairs = []
    p = 1
    while p < n:
        k = p
        while k >= 1:
            for j in range(k % p, n - k, 2 * k):
                for i in range(min(k, n - j - k)):
                    if (i + j) // (2 * p) == (i + j + k) // (2 * p):
                        pairs.append((i + j, i + j + k))
            k //= 2
        p *= 2
    return pairs


def _topk_desc(sc, k):
    n = sc.shape[0] // SUBLANES
    rows = [sc[SUBLANES * i:SUBLANES * (i + 1)] for i in range(n)]
    for i, j in _oddeven_merge_sort_pairs(n):
        rows[i], rows[j] = jnp.maximum(rows[i], rows[j]), jnp.minimum(rows[i], rows[j])
    vals = []
    for it in range(k):
        m = jnp.max(rows[0], axis=0, keepdims=True)
        vals.append(m)
        if it + 1 < k:
            win = rows[0] == m
            depth = min(n, k - it)
            for r in range(depth - 1):
                rows[r] = jnp.where(win, rows[r + 1], rows[r])
            rows[depth - 1] = jnp.where(win, NEG_INF, rows[depth - 1])
    return vals


def _outproj_peer_kernel(x_ref, a0_ref, a1_ref, a2_ref, gm_ref, gt1_ref, sh2_ref, sc2_ref, g1_ref, g2_ref,
                         wo_ref, wq_ref, keys_ref,
                         x1_ref, h2t_ref, s1_ref, s2_ref, e2_ref, thr_ref, lse_ref):
    mo = _dot(a0_ref[...], wo_ref[0:GROUP_W, :])
    mo += _dot(a1_ref[...], wo_ref[GROUP_W:2 * GROUP_W, :])
    mo += _dot(a2_ref[...], wo_ref[2 * GROUP_W:3 * GROUP_W, :])
    mo += _dot(gm_ref[...], wo_ref[3 * GROUP_W:4 * GROUP_W, :])
    x1 = x_ref[...] + gt1_ref[...] * _rms(mo, g1_ref[...])
    x1_ref[...] = x1
    h2 = _rms(x1, g2_ref[...]) * (1.0 + sc2_ref[...]) + sh2_ref[...]
    h2t_ref[0] = jnp.transpose(h2).astype(BF16)
    qr = _dot(h2.astype(BF16), wq_ref[...])
    lane_groups = thr_ref.shape[0]
    for h in range(PEER_HEADS):
        tops = []
        for p, s_ref in enumerate((s1_ref, s2_ref)):
            c0 = (h * 2 + p) * PEER_HALF
            sc = _dot_nt(keys_ref[p], qr[:, c0:c0 + PEER_HALF].astype(BF16)) * LOG2E
            for lg in range(lane_groups):
                s_ref[lg, h] = sc[:, lg * LANES:(lg + 1) * LANES]
            tops.append(_topk_desc(sc, PEER_TOPK))
            if p == 1:
                e2 = jnp.exp2(sc - tops[1][0])
                for lg in range(lane_groups):
                    e2_ref[lg, h] = e2[:, lg * LANES:(lg + 1) * LANES]
        t1 = jnp.concatenate(tops[0], axis=0)
        half = PEER_TOPK // 2
        cand = jnp.concatenate(
            [t1 + tops[1][0]]
            + [t1[:half] + tops[1][b] for b in range(1, half)]
            + [jnp.concatenate(tops[1][half:], axis=0) + tops[0][0]], axis=0)
        best = _topk_desc(cand, PEER_TOPK)
        z = jnp.ones_like(best[0])
        for v in best[1:]:
            z = z + jnp.exp2(v - best[0])
        lse = best[0] + jnp.log(z) * LOG2E - tops[1][0]
        for lg in range(lane_groups):
            thr_ref[lg, pl.ds(h, 1), :] = best[-1][:, lg * LANES:(lg + 1) * LANES]
            lse_ref[lg, pl.ds(h, 1), :] = lse[:, lg * LANES:(lg + 1) * LANES]


def _outproj_peer(x, atts, gm, mods, norm_g, w_out, w_q, keys, tm, per_row_mod, tiles_per_seq):
    n = x.shape[0]
    if per_row_mod:
        mod_specs = [pl.BlockSpec((tm, D_MODEL), lambda i: (i, 0))] * 3
    else:
        mod_specs = [pl.BlockSpec((None, 1, D_MODEL), lambda i, k=k: ((i // tiles_per_seq) * 6 + k, 0, 0))
                     for k in (2, 3, 4)]
    row_spec = lambda w: pl.BlockSpec((tm, w), lambda i: (i, 0))
    const = lambda shape: pl.BlockSpec(shape, lambda i: tuple(0 for _ in shape))
    return pl.pallas_call(
        _outproj_peer_kernel,
        grid=(n // tm,),
        in_specs=[row_spec(D_MODEL)] + [row_spec(GROUP_W)] * 4 + mod_specs
        + [pl.BlockSpec((None, 1, D_MODEL), lambda i: (1, 0, 0)),
           pl.BlockSpec((None, 1, D_MODEL), lambda i: (2, 0, 0)),
           const((D_MODEL, D_MODEL)), const((D_MODEL, D_MODEL)),
           const((2, N_KEYS, PEER_HALF))],
        out_specs=[row_spec(D_MODEL),
                   pl.BlockSpec((1, D_MODEL, tm), lambda i: (i, 0, 0)),
                   pl.BlockSpec((tm // LANES, PEER_HEADS, N_KEYS, LANES), lambda i: (i, 0, 0, 0)),
                   pl.BlockSpec((tm // LANES, PEER_HEADS, N_KEYS, LANES), lambda i: (i, 0, 0, 0)),
                   pl.BlockSpec((tm // LANES, PEER_HEADS, N_KEYS, LANES), lambda i: (i, 0, 0, 0)),
                   pl.BlockSpec((tm // LANES, PEER_HEADS, LANES), lambda i: (i, 0, 0)),
                   pl.BlockSpec((tm // LANES, PEER_HEADS, LANES), lambda i: (i, 0, 0))],
        out_shape=[jax.ShapeDtypeStruct((n, D_MODEL), F32),
                   jax.ShapeDtypeStruct((n // tm, D_MODEL, tm), BF16),
                   jax.ShapeDtypeStruct((n // LANES, PEER_HEADS, N_KEYS, LANES), F32),
                   jax.ShapeDtypeStruct((n // LANES, PEER_HEADS, N_KEYS, LANES), F32),
                   jax.ShapeDtypeStruct((n // LANES, PEER_HEADS, N_KEYS, LANES), F32),
                   jax.ShapeDtypeStruct((n // LANES, PEER_HEADS, LANES), F32),
                   jax.ShapeDtypeStruct((n // LANES, PEER_HEADS, LANES), F32)],
        compiler_params=_cparams(("parallel",)),
        name="outproj_peer",
    )(x, *atts, gm, *mods, norm_g, norm_g, w_out, w_q, keys)


WEIGHT_BUFS = 3


def _expert_kernel(h2t_ref, u_hbm, vt_hbm, s1_ref, s2_ref, e2_ref, thr_ref, lse_ref, x1_ref, gt2_ref, g3_ref,
                   y_ref, acc_scr, g_scr, w_scr, rowb_scr, thrb_scr, tile_scr, u_buf, vt_buf, w_sem):
    j = pl.program_id(1)
    n_steps = pl.num_programs(1)
    n_tiles = n_steps - 1
    te = u_buf.shape[1]
    rows_per_tile = te // N_KEYS

    step = pl.program_id(0) * n_steps + j
    total = pl.num_programs(0) * n_steps

    def weight_copies(s):
        tile = jnp.maximum(lax.rem(s, n_steps) - 1, 0)
        slot = lax.rem(s, WEIGHT_BUFS)
        return (pltpu.make_async_copy(u_hbm.at[tile], u_buf.at[slot], w_sem.at[0, slot]),
                pltpu.make_async_copy(vt_hbm.at[tile], vt_buf.at[slot], w_sem.at[1, slot]))

    @pl.when(step == 0)
    def _():
        for s0 in range(WEIGHT_BUFS - 1):
            for c in weight_copies(jnp.int32(s0)):
                c.start()

    @pl.when(step + WEIGHT_BUFS - 1 < total)
    def _():
        for c in weight_copies(step + WEIGHT_BUFS - 1):
            c.start()

    for c in weight_copies(step):
        c.wait()
    u_ref = u_buf.at[lax.rem(step, WEIGHT_BUFS)]
    vt_ref = vt_buf.at[lax.rem(step, WEIGHT_BUFS)]
    lane_groups = s1_ref.shape[0]
    n_chunks, _, chunk_w = h2t_ref.shape
    lg_per_chunk = chunk_w // LANES
    n_rows = rows_per_tile * PEER_HEADS
    ch = 8

    @pl.when(j == 0)
    def _():
        acc_scr[...] = jnp.zeros_like(acc_scr)
        g_scr[...] = jnp.zeros_like(g_scr)
        for lg in range(lane_groups):
            for h in range(PEER_HEADS):
                thrb_scr[lg, h] = jnp.broadcast_to(thr_ref[lg, pl.ds(h, 1), :], (ch, LANES))

    jt = jnp.minimum(j, n_tiles - 1)
    for lg in range(lane_groups):
        for rr in range(rows_per_tile):
            tile = tile_scr.at[lg * rows_per_tile + rr]
            for h in range(PEER_HEADS):
                tile[pl.ds(h, 1), :] = s1_ref[lg, h, pl.ds(jt * rows_per_tile + rr, 1), :]
            rows8 = tile[...]
            e8 = jnp.exp2(rows8 - lse_ref[lg])
            for h in range(PEER_HEADS):
                rowb_scr[lg, rr * PEER_HEADS + h] = jnp.broadcast_to(rows8[h:h + 1], (ch, LANES))
                rowb_scr[lg, n_rows + rr * PEER_HEADS + h] = jnp.broadcast_to(e8[h:h + 1], (ch, LANES))

    def build_gates(lg, part=0, n_parts=1):
        per_part = N_KEYS // ch // n_parts
        for ci in range(per_part):
            cc = part * per_part + ci
            c0 = cc * ch if n_parts == 1 else pl.multiple_of(cc * ch, ch)
            gates = [None] * rows_per_tile
            for h in range(PEER_HEADS):
                b = s2_ref[lg, h, pl.ds(c0, ch), :]
                eb = e2_ref[lg, h, pl.ds(c0, ch), :]
                t = thrb_scr[lg, h]
                for rr in range(rows_per_tile):
                    a = rowb_scr[lg, rr * PEER_HEADS + h]
                    ea = rowb_scr[lg, n_rows + rr * PEER_HEADS + h]
                    term = jnp.where(a + b >= t, ea * eb, 0.0)
                    gates[rr] = term if gates[rr] is None else gates[rr] + term
            for rr in range(rows_per_tile):
                g_scr[lg, pl.ds(rr * N_KEYS + c0, ch), :] = gates[rr]

    for k in range(n_chunks):
        act = _dot(u_ref[...], h2t_ref[k])
        for q in range(lg_per_chunk):
            lg = k * lg_per_chunk + q
            w_scr[:, lg * LANES:(lg + 1) * LANES] = (
                g_scr[lg] * _gelu_sigmoid(act[:, q * LANES:(q + 1) * LANES])).astype(BF16)

    parts = 2 if lane_groups == 1 else 1
    d_slab = acc_scr.shape[0] // (lane_groups * parts)

    def piece(p, carry):
        d0 = pl.multiple_of(p * d_slab, d_slab)
        acc_scr[pl.ds(d0, d_slab), :] += _dot(vt_ref[pl.ds(d0, d_slab), :], w_scr[...])
        if parts == 1:
            build_gates(p)
        else:
            build_gates(0, p, parts)
        return carry

    lax.fori_loop(0, lane_groups * parts, piece, 0)

    @pl.when(j == n_tiles)
    def _():
        peer = jnp.transpose(acc_scr[...])
        y_ref[...] = x1_ref[...] + gt2_ref[...] * _rms(peer, g3_ref[...])


def _experts(h2t, u_bf, vt_bf, s1, s2, e2, thr, lse, x1, gt2, norm_g, tm, per_row_mod, tiles_per_seq):
    chunk_w = h2t.shape[2]
    n = h2t.shape[0] * chunk_w
    n_tiles, _, te = vt_bf.shape
    prev_tile = lambda j: jnp.maximum(j - 1, 0)
    if per_row_mod:
        gt2_spec = pl.BlockSpec((tm, D_MODEL), lambda i, j: (i, 0))
    else:
        gt2_spec = pl.BlockSpec((None, 1, D_MODEL), lambda i, j: ((i // tiles_per_seq) * 6 + 5, 0, 0))
    return pl.pallas_call(
        _expert_kernel,
        grid=(n // tm, n_tiles + 1),
        in_specs=[
            pl.BlockSpec((tm // chunk_w, D_MODEL, chunk_w), lambda i, j: (i, 0, 0)),
            pl.BlockSpec(memory_space=pl.ANY),
            pl.BlockSpec(memory_space=pl.ANY),
            pl.BlockSpec((tm // LANES, PEER_HEADS, N_KEYS, LANES), lambda i, j: (i, 0, 0, 0)),
            pl.BlockSpec((tm // LANES, PEER_HEADS, N_KEYS, LANES), lambda i, j: (i, 0, 0, 0)),
            pl.BlockSpec((tm // LANES, PEER_HEADS, N_KEYS, LANES), lambda i, j: (i, 0, 0, 0)),
            pl.BlockSpec((tm // LANES, PEER_HEADS, LANES), lambda i, j: (i, 0, 0)),
            pl.BlockSpec((tm // LANES, PEER_HEADS, LANES), lambda i, j: (i, 0, 0)),
            pl.BlockSpec((tm, D_MODEL), lambda i, j: (i, 0)),
            gt2_spec,
            pl.BlockSpec((None, 1, D_MODEL), lambda i, j: (3, 0, 0)),
        ],
        out_specs=pl.BlockSpec((tm, D_MODEL), lambda i, j: (i, 0)),
        out_shape=jax.ShapeDtypeStruct((n, D_MODEL), F32),
        scratch_shapes=[pltpu.VMEM((D_MODEL, tm), F32), pltpu.VMEM((tm // LANES, te, LANES), F32),
                        pltpu.VMEM((te, tm), BF16),
                        pltpu.VMEM((tm // LANES, 2 * (te // N_KEYS) * PEER_HEADS, 8, LANES), F32),
                        pltpu.VMEM((tm // LANES, PEER_HEADS, 8, LANES), F32),
                        pltpu.VMEM((tm // LANES * (te // N_KEYS), PEER_HEADS, LANES), F32),
                        pltpu.VMEM((WEIGHT_BUFS, te, D_MODEL), BF16), pltpu.VMEM((WEIGHT_BUFS, D_MODEL, te), BF16),
                        pltpu.SemaphoreType.DMA((2, WEIGHT_BUFS))],
        compiler_params=_cparams(("arbitrary", "arbitrary")),
        name="peer_experts",
    )(h2t, u_bf.reshape(n_tiles, te, D_MODEL), vt_bf, s1, s2, e2, thr, lse, x1, gt2, norm_g)


def _cast_transpose_kernel(v_ref, o_ref):
    tiles, _, te = o_ref.shape
    for k in range(tiles):
        o_ref[k] = jnp.transpose(v_ref[k * te:(k + 1) * te, :]).astype(BF16)


def _cast_transpose(v, te):
    rows, cols = v.shape
    per_step = 2
    return pl.pallas_call(
        _cast_transpose_kernel,
        grid=(rows // (per_step * te),),
        in_specs=[pl.BlockSpec((per_step * te, cols), lambda i: (i, 0))],
        out_specs=pl.BlockSpec((per_step, cols, te), lambda i: (i, 0, 0)),
        out_shape=jax.ShapeDtypeStruct((rows // te, cols, te), BF16),
        compiler_params=_cparams(("parallel",)),
        name="cast_transpose",
    )(v)


def _permute_w_in(w_in):
    q = w_in[:, :ATT_WIDTH]
    k = w_in[:, ATT_WIDTH:2 * ATT_WIDTH]
    v = w_in[:, 2 * ATT_WIDTH:3 * ATT_WIDTH]
    parts = [q]
    for g in range(N_DIL):
        parts += [k[:, g * GROUP_W:(g + 1) * GROUP_W], v[:, g * GROUP_W:(g + 1) * GROUP_W]]
    parts.append(w_in[:, 3 * ATT_WIDTH:])
    return jnp.concatenate(parts, axis=1).astype(BF16)


def kernel(x_prompt, x_sample, c_prompt, c_sample, cache_kv_dil1, cache_kv_dil4, cache_kv_dil16, w_ada, b_ada, norm_g, w_in, gm_ln_g, gm_ln_b, w_s, b_s, w_out, w_q, sub_keys, expert_u, expert_v):
    batch, seq, _ = x_prompt.shape
    n_s = x_sample.shape[0] * x_sample.shape[1]
    assert w_ada.shape[0] == 1 and x_sample.shape[1] == 1
    n_p = batch * seq
    pad_s = 128

    w_in_p = _permute_w_in(w_in[0])
    w_out_b = w_out[0].astype(BF16)
    w_q_b = w_q[0].astype(BF16)
    keys_b = sub_keys[0].astype(BF16)
    u_b = expert_u[0].astype(BF16)
    vt_b = _cast_transpose(expert_v[0], 512)
    ng = norm_g[0].reshape(4, 1, D_MODEL)
    ln_g, ln_b = gm_ln_g[0], gm_ln_b[0]

    rows = batch + n_s
    c_all = jnp.zeros((16, D_MODEL), F32).at[:rows].set(jnp.concatenate([c_prompt, c_sample], axis=0))
    mod = _ada(c_all, w_ada[0], b_ada)
    mod_p = mod.reshape(16 * 6, 1, D_MODEL)
    mod_s = jnp.zeros((pad_s, 6, D_MODEL), F32).at[:n_s].set(mod[batch:rows].reshape(n_s, 6, D_MODEL))
    mods_s = [mod_s[:, k] for k in range(6)]

    xp = x_prompt.reshape(n_p, D_MODEL)
    bsb = jnp.broadcast_to(b_s[0][:, :, None], (GM_HEADS, CHUNK, GM_HD))
    outs = _inproj_prompt(xp, mod_p, ng, w_in_p, w_s[0], bsb, ln_g, ln_b, batch, seq)
    folded, kv_p, gm_p = outs[:9], outs[9:12], outs[12]
    att_p = _attn_prompt(folded, batch, seq)
    tm_p = 256
    x1_p, h2t_p, *sel_p = _outproj_peer(
        xp, att_p, gm_p, [mod_p] * 3, ng, w_out_b, w_q_b, keys_b, tm_p, False, seq // tm_p)
    tm_e = 512
    y_p = _experts(h2t_p, u_b, vt_b, *sel_p, x1_p, mod_p, ng, tm_e, False, seq // tm_e)

    xs = jnp.zeros((pad_s, D_MODEL), F32).at[:n_s].set(x_sample.reshape(n_s, D_MODEL))
    wd = jnp.repeat(w_s[0][:, 0, 0], GM_HD)[None, :]
    b0 = jnp.repeat(b_s[0][:, 0], GM_HD)[None, :]
    q_s, kv0_s, kv1_s, kv2_s, gm_s, gvn_s = _inproj_sample(
        xs, mods_s[0], mods_s[1], ng, w_in_p, wd, b0, ln_g, ln_b)
    new_kv_s = [kv[:n_s].reshape(n_s, 2, ATT_HEADS, ATT_HD) for kv in (kv0_s, kv1_s, kv2_s)]
    caches_t = [jnp.transpose(c[0], (0, 2, 3, 4, 1)) for c in (cache_kv_dil1, cache_kv_dil4, cache_kv_dil16)]
    att_s = _attn_sample(q_s[:n_s].reshape(n_s, N_DIL, ATT_HEADS, ATT_HD, 1),
                         [kv[..., None] for kv in new_kv_s], caches_t)
    att_s = att_s.reshape(n_s, N_DIL, GROUP_W).astype(BF16)
    att_s = [jnp.zeros((pad_s, GROUP_W), BF16).at[:n_s].set(att_s[:, g]) for g in range(N_DIL)]
    x1_s, h2t_s, *sel_s = _outproj_peer(
        xs, att_s, gm_s, mods_s[2:5], ng, w_out_b, w_q_b, keys_b, pad_s, True, 1)
    y_s = _experts(h2t_s, u_b, vt_b, *sel_s, x1_s, mods_s[5], ng, pad_s, True, 1)

    y_prompt = y_p.reshape(batch, seq, D_MODEL)
    y_sample = y_s[:n_s].reshape(x_sample.shape)
    new_kv_p = []
    for kv_t, (win, _) in zip(kv_p, DIL_PAIRS):
        keep = min(win, seq)
        kv_t = kv_t[:, :, seq - keep:].reshape(batch, 2, ATT_HEADS, ATT_HD, keep)
        new_kv_p.append(jnp.transpose(kv_t, (0, 4, 1, 2, 3))[None])
    new_kv_s = [kv.reshape(1, n_s, 1, 2, ATT_HEADS, ATT_HD) for kv in new_kv_s]
    state_gv = gvn_s[:n_s].reshape(1, n_s, 1, GM_HEADS, GM_HD)
    return (y_prompt, y_sample, *new_kv_p, *new_kv_s, state_gv)
```
